```python
import math
import jax, jax.numpy as jnp
from jax import lax
import numpy as np


D_MODEL = 2048
BATCH = 8
SEQ = 2048
DEPTH = 2
DEC_BATCH = 1
DEC_SEQ = 16384
PAST_LEN = 128

HEAD_DIM = 128
N_DIFF_HEADS = 8
DIFF_HALF = HEAD_DIM // 2
N_WIN_HEADS = 8
N_WIN_KV = 2
WIN_GROUP = N_WIN_HEADS // N_WIN_KV
WINDOW = 128
BLOCK = 128
MIX_WIDTH = (N_DIFF_HEADS + N_WIN_HEADS) * HEAD_DIM
DQ = N_DIFF_HEADS * HEAD_DIM
WQ = N_WIN_HEADS * HEAD_DIM
WKV = N_WIN_KV * HEAD_DIM
IN_COLS = 3 * DQ + WQ + 2 * WKV
N_GROUPS = 4
EXPERTS_PER_GROUP = 8
N_EXPERTS = N_GROUPS * EXPERTS_PER_GROUP
TOP_K_IN_GROUP = 2
D_EXPERT = 512
MOE_CHUNK = 128
ALPHA = (2.0 * DEPTH) ** 0.25
BETA = (8.0 * DEPTH) ** -0.25
LN_EPS = 1e-5
RMS_EPS = 1e-5

kernel_name = 'hymba_diff_window_hmoe_deepnorm_encoder'


def alibi_slopes(n):
    return jnp.asarray([2.0 ** (-8.0 * (h + 1) / n) for h in range(n)], jnp.float32)


def layer_norm(x, g, b):
    xf = x.astype(jnp.float32)
    mu = xf.mean(-1, keepdims=True)
    var = jnp.square(xf - mu).mean(-1, keepdims=True)
    return ((xf - mu) * lax.rsqrt(var + LN_EPS)).astype(x.dtype) * g + b


def diff_attention(q, k, v, lam_vec, lam_init, subln_g):
    B, S, H, _, Dh = q.shape
    nb = S // BLOCK
    scale = Dh ** -0.5
    slopes = alibi_slopes(H)
    lv = lam_vec.astype(jnp.float32)
    lam = jnp.exp(jnp.sum(lv[0] * lv[1])) - jnp.exp(jnp.sum(lv[2] * lv[3])) + lam_init
    kpos = jnp.arange(S)
    qb = q.reshape(B, nb, BLOCK, H, 2, Dh).transpose(1, 0, 2, 3, 4, 5)

    def one_block(args):
        qblk, i = args
        qpos = i * BLOCK + jnp.arange(BLOCK)
        dist = jnp.abs(qpos[:, None] - kpos[None, :]).astype(jnp.float32)
        s = jnp.einsum('bqhcd,bshcd->bhcqs', qblk, k, preferred_element_type=jnp.float32) * scale
        s = s - slopes[None, :, None, None, None] * dist[None, None, None]
        p = jax.nn.softmax(s, axis=-1)
        a = p[:, :, 0] - lam * p[:, :, 1]
        return jnp.einsum('bhqs,bshd->bqhd', a.astype(v.dtype), v)

    o = lax.map(one_block, (qb, jnp.arange(nb)))
    o = o.transpose(1, 0, 2, 3, 4).reshape(B, S, H, v.shape[-1])
    of = o.astype(jnp.float32)
    of = of * lax.rsqrt(jnp.mean(jnp.square(of), -1, keepdims=True) + RMS_EPS) * (1.0 - lam_init)
    return of.astype(v.dtype) * subln_g


def window_attention(q, k, v, sink):
    B, S, H, Dh = q.shape
    nb = S // BLOCK
    scale = Dh ** -0.5
    slopes = alibi_slopes(H).reshape(N_WIN_KV, WIN_GROUP)
    pad = ((0, 0), (BLOCK, BLOCK), (0, 0), (0, 0))
    kp = jnp.pad(k, pad).reshape(B, nb + 2, BLOCK, N_WIN_KV, Dh)
    vp = jnp.pad(v, pad).reshape(B, nb + 2, BLOCK, N_WIN_KV, Dh)
    kb = jnp.concatenate([kp[:, :-2], kp[:, 1:-1], kp[:, 2:]], axis=2)
    vb = jnp.concatenate([vp[:, :-2], vp[:, 1:-1], vp[:, 2:]], axis=2)
    qb = q.reshape(B, nb, BLOCK, N_WIN_KV, WIN_GROUP, Dh)
    s = jnp.einsum('bnqkgd,bnskd->bnkgqs', qb, kb, preferred_element_type=jnp.float32) * scale
    qpos = jnp.arange(nb)[:, None] * BLOCK + jnp.arange(BLOCK)[None, :]
    kpos = jnp.arange(nb)[:, None] * BLOCK - BLOCK + jnp.arange(3 * BLOCK)[None, :]
    rel = jnp.abs(qpos[:, :, None] - kpos[:, None, :])
    valid = (rel <= WINDOW) & (kpos[:, None, :] >= 0) & (kpos[:, None, :] < S)
    s = s - slopes[None, None, :, :, None, None] * rel.astype(jnp.float32)[None, :, None, None]
    s = jnp.where(valid[None, :, None, None], s, -jnp.inf)
    sink_l = sink.astype(jnp.float32).reshape(N_WIN_KV, WIN_GROUP)[None, None, :, :, None, None]
    m = jnp.maximum(s.max(-1, keepdims=True), sink_l)
    p = jnp.exp(s - m)
    p = p / (p.sum(-1, keepdims=True) + jnp.exp(sink_l - m))
    o = jnp.einsum('bnkgqs,bnskd->bnqkgd', p.astype(v.dtype), vb)
    return o.reshape(B, S, H, Dh)


def hier_moe(h, w_rg, b_rg, w_re, b_re, w_gate, w_up, w_down):
    T, D = h.shape
    hc = h.reshape(T // MOE_CHUNK, MOE_CHUNK, D)

    def chunk(hx):
        gl = jnp.matmul(hx, w_rg, preferred_element_type=jnp.float32) + b_rg.astype(jnp.float32)
        gp = jax.nn.softmax(gl, axis=-1)
        g_idx = jnp.argmax(gl, axis=-1)
        g_w = jnp.take_along_axis(gp, g_idx[:, None], axis=-1)
        el = jnp.einsum('td,gde->tge', hx, w_re, preferred_element_type=jnp.float32) + b_re.astype(jnp.float32)
        el = jnp.take_along_axis(el, g_idx[:, None, None], axis=1)[:, 0]
        ev, e_idx = lax.top_k(el, TOP_K_IN_GROUP)
        ew = jax.nn.softmax(ev, axis=-1) * g_w
        gidx = g_idx[:, None] * EXPERTS_PER_GROUP + e_idx
        combine = jnp.sum(jax.nn.one_hot(gidx, N_EXPERTS, dtype=jnp.float32) * ew[..., None], axis=1)
        a = jax.nn.silu(jnp.einsum('td,edf->tef', hx, w_gate)) * jnp.einsum('td,edf->tef', hx, w_up)
        a = a * combine[:, :, None].astype(a.dtype)
        return jnp.einsum('tef,efd->td', a, w_down)

    return lax.map(chunk, hc).reshape(T, D)


def encoder_layer(x, c, l, w_ada, b_ada, w_in, lam, subln_g, sink, w_o, ln_g, ln_b,
                  w_rg, b_rg, w_re, b_re, w_gate, w_up, w_down):
    B, S, D = x.shape
    mod = jnp.matmul(jax.nn.silu(c), w_ada[l]) + b_ada[l]
    sh1, sc1, g1, sh2, sc2, g2 = jnp.split(mod, 6, axis=-1)
    h = x * (1.0 + sc1[:, None]) + sh1[:, None]
    proj = jnp.matmul(h, w_in[l])
    o0 = 0
    q_d = proj[..., o0:o0 + DQ].reshape(B, S, N_DIFF_HEADS, 2, DIFF_HALF); o0 += DQ
    k_d = proj[..., o0:o0 + DQ].reshape(B, S, N_DIFF_HEADS, 2, DIFF_HALF); o0 += DQ
    v_d = proj[..., o0:o0 + DQ].reshape(B, S, N_DIFF_HEADS, HEAD_DIM); o0 += DQ
    q_w = proj[..., o0:o0 + WQ].reshape(B, S, N_WIN_HEADS, HEAD_DIM); o0 += WQ
    k_w = proj[..., o0:o0 + WKV].reshape(B, S, N_WIN_KV, HEAD_DIM); o0 += WKV
    v_w = proj[..., o0:o0 + WKV].reshape(B, S, N_WIN_KV, HEAD_DIM)
    lam_init = 0.8 - 0.6 * math.exp(-0.3 * l)
    out_a = diff_attention(q_d, k_d, v_d, lam[l], lam_init, subln_g[l])
    out_b = window_attention(q_w, k_w, v_w, sink[l])
    heads = jnp.concatenate([out_a.reshape(B, S, DQ), out_b.reshape(B, S, WQ)], axis=-1)
    att = jnp.matmul(heads, w_o[l])
    x = layer_norm(ALPHA * x + g1[:, None] * att, ln_g[l, 0], ln_b[l, 0])
    h = x * (1.0 + sc2[:, None]) + sh2[:, None]
    y = hier_moe(h.reshape(B * S, D), w_rg[l], b_rg[l], w_re[l], b_re[l],
                 w_gate[l], w_up[l], w_down[l]).reshape(B, S, D)
    x = layer_norm(ALPHA * x + g2[:, None] * y, ln_g[l, 1], ln_b[l, 1])
    return x


def setup_inputs(seed: int = 0) -> dict:
    key = jax.random.key(seed)
    ks = jax.random.split(key, 24)
    n = lambda k, shape: jax.random.normal(k, shape, jnp.float32)
    D = D_MODEL
    col_scale = jnp.ones((IN_COLS,), jnp.float32)
    col_scale = col_scale.at[2 * DQ:3 * DQ].set(BETA).at[IN_COLS - WKV:].set(BETA)
    return {
        'x_prompt': n(ks[0], (BATCH, SEQ, D)),
        'x_sample': n(ks[1], (DEC_BATCH, DEC_SEQ, D)),
        'c_prompt': n(ks[2], (BATCH, D)),
        'c_sample': n(ks[3], (DEC_BATCH, D)),
        'w_ada': n(ks[4], (DEPTH, D, 6 * D)) * D ** -0.5,
        'b_ada': n(ks[5], (DEPTH, 6 * D)) * 0.02,
        'w_in': n(ks[6], (DEPTH, D, IN_COLS)) * D ** -0.5 * col_scale,
        'lam': n(ks[7], (DEPTH, 4, DIFF_HALF)) * 0.1,
        'subln_g': 1.0 + 0.02 * n(ks[8], (DEPTH, HEAD_DIM)),
        'sink': n(ks[9], (DEPTH, N_WIN_HEADS)) * 0.5,
        'w_o': n(ks[10], (DEPTH, MIX_WIDTH, D)) * MIX_WIDTH ** -0.5 * BETA,
        'ln_g': 1.0 + 0.02 * n(ks[11], (DEPTH, 2, D)),
        'ln_b': 0.02 * n(ks[12], (DEPTH, 2, D)),
        'w_rg': n(ks[13], (DEPTH, D, N_GROUPS)) * D ** -0.5,
        'b_rg': n(ks[14], (DEPTH, N_GROUPS)) * 0.01,
        'w_re': n(ks[15], (DEPTH, N_GROUPS, D, EXPERTS_PER_GROUP)) * D ** -0.5,
        'b_re': n(ks[16], (DEPTH, N_GROUPS, EXPERTS_PER_GROUP)) * 0.01,
        'w_gate': n(ks[17], (DEPTH, N_EXPERTS, D, D_EXPERT)) * D ** -0.5,
        'w_up': n(ks[18], (DEPTH, N_EXPERTS, D, D_EXPERT)) * D ** -0.5 * BETA,
        'w_down': n(ks[19], (DEPTH, N_EXPERTS, D_EXPERT, D)) * D_EXPERT ** -0.5 * BETA,
    }


def reference(x_prompt, x_sample, c_prompt, c_sample, w_ada, b_ada, w_in, lam, subln_g, sink,
              w_o, ln_g, ln_b, w_rg, b_rg, w_re, b_re, w_gate, w_up, w_down):
    yp = x_prompt
    ys = x_sample
    for l in range(DEPTH):
        yp = encoder_layer(yp, c_prompt, l, w_ada, b_ada, w_in, lam, subln_g, sink, w_o, ln_g, ln_b,
                           w_rg, b_rg, w_re, b_re, w_gate, w_up, w_down)
        ys = encoder_layer(ys, c_sample, l, w_ada, b_ada, w_in, lam, subln_g, sink, w_o, ln_g, ln_b,
                           w_rg, b_rg, w_re, b_re, w_gate, w_up, w_down)
    return (yp, ys)
```

```python
import functools
import math

import jax
import jax.numpy as jnp
from jax import lax
from jax.experimental import pallas as pl
from jax.experimental.pallas import tpu as pltpu

F32 = jnp.float32
BF16 = jnp.bfloat16

HEAD_DIM = 128
DIFF_HALF = HEAD_DIM // 2
N_DIFF_HEADS = 8
N_WIN_HEADS = 8
N_WIN_KV = 2
WIN_GROUP = N_WIN_HEADS // N_WIN_KV
WINDOW = 128
DQ = N_DIFF_HEADS * HEAD_DIM
WQ = N_WIN_HEADS * HEAD_DIM
WKV = N_WIN_KV * HEAD_DIM
N_GROUPS = 4
EXPERTS_PER_GROUP = 8
N_EXPERTS = N_GROUPS * EXPERTS_PER_GROUP
LN_EPS = 1e-5
RMS_EPS = 1e-5
LOG2E = 1.4426950408889634
NEG_BIG = -1e30

V7X_VMEM_BYTES = 64 * 1024 * 1024
VMEM_LIMIT = V7X_VMEM_BYTES - 8 * 1024 * 1024
LANES = 128
SUBLANES = 8

_NT = (((1,), (1,)), ((), ()))


def _cparams(n_axes):
    return pltpu.CompilerParams(
        dimension_semantics=("arbitrary",) * n_axes, vmem_limit_bytes=VMEM_LIMIT)


def _pick(n, pref):
    t = min(pref, n)
    while n % t:
        t //= 2
    return t


def _adaln_kernel(c_ref, w_ref, b_ref, o_ref):
    c = c_ref[...]
    s = (c * jax.nn.sigmoid(c)).astype(BF16)
    o_ref[...] = jnp.dot(s, w_ref[...].astype(BF16), preferred_element_type=F32) + b_ref[...]


def _adaln(c_pad, w_ada, b_ada):
    depth, d, n = w_ada.shape
    tn = _pick(n, 1024)
    rows = c_pad.shape[0]
    return pl.pallas_call(
        _adaln_kernel,
        out_shape=jax.ShapeDtypeStruct((depth, rows, n), F32),
        grid=(depth, n // tn),
        in_specs=[
            pl.BlockSpec((rows, d), lambda l, j: (0, 0)),
            pl.BlockSpec((None, d, tn), lambda l, j: (l, 0, j)),
            pl.BlockSpec((None, 1, tn), lambda l, j: (l, 0, j)),
        ],
        out_specs=pl.BlockSpec((None, rows, tn), lambda l, j: (l, 0, j)),
        compiler_params=_cparams(2),
        name="adaln_mod",
    )(c_pad, w_ada, b_ada.reshape(depth, 1, n))


class _Layout:
    def __init__(self, bp, sp, bs, ss):
        self.bp, self.sp, self.bs, self.ss = bp, sp, bs, ss
        self.tp = bp * sp
        self.t = self.tp + bs * ss

    def batch_of_tile(self, i, tm):
        row = i * tm
        return jnp.where(row < self.tp, row // self.sp, self.bp + (row - self.tp) // self.ss)


def _mod_spec(lay, tm, chunk, d):
    return pl.BlockSpec((None, 1, d), lambda i: (lay.batch_of_tile(i, tm), 0, chunk))


def _x_specs(lay, tm, d, same_array):
    npb = lay.tp // tm
    boff = npb if same_array else 0
    return [
        pl.BlockSpec((tm, d), lambda i: (jnp.minimum(i, npb - 1), 0)),
        pl.BlockSpec((tm, d), lambda i: (jnp.maximum(i - npb, 0) + boff, 0)),
    ]


def _select_x(i, npb, xa_ref, xb_ref):
    return jnp.where(i < npb, xa_ref[...], xb_ref[...])


def _inproj_kernel(xa_ref, xb_ref, sc_ref, sh_ref, w_ref, wvt_ref, cs_ref, proj_ref, vt_ref, *, npb):
    x = _select_x(pl.program_id(0), npb, xa_ref, xb_ref)
    h = (x * (1.0 + sc_ref[...]) + sh_ref[...]).astype(BF16)
    acc = jnp.dot(h, w_ref[...], preferred_element_type=F32)
    proj_ref[...] = (acc * cs_ref[...]).astype(BF16)
    vt = lax.dot_general(wvt_ref[...], h, _NT, preferred_element_type=F32)
    vt_ref[...] = vt.astype(BF16)


def _inproj(lay, xa, xb, same_array, mod3, w_main, wvt, col_scale, tm):
    d = xa.shape[1]
    n = w_main.shape[1]
    nv = wvt.shape[0]
    const = dict(pipeline_mode=pl.Buffered(1))
    return pl.pallas_call(
        functools.partial(_inproj_kernel, npb=lay.tp // tm),
        out_shape=(jax.ShapeDtypeStruct((lay.t, n), BF16),
                   jax.ShapeDtypeStruct((lay.t // tm, nv, tm), BF16)),
        grid=(lay.t // tm,),
        in_specs=_x_specs(lay, tm, d, same_array) + [
            _mod_spec(lay, tm, 1, d),
            _mod_spec(lay, tm, 0, d),
            pl.BlockSpec((d, n), lambda i: (0, 0), **const),
            pl.BlockSpec((nv, d), lambda i: (0, 0), **const),
            pl.BlockSpec((1, n), lambda i: (0, 0)),
        ],
        out_specs=(pl.BlockSpec((tm, n), lambda i: (i, 0)),
                   pl.BlockSpec((None, nv, tm), lambda i: (i, 0, 0))),
        compiler_params=_cparams(1),
        name="in_proj",
    )(xa, xb, mod3, mod3, w_main, wvt, col_scale)


def _diff_attn_kernel(sl_ref, q_ref, k_ref, vt_ref, lam_ref, g_ref, o_ref, acc1, acc2,
                      *, s_len, tq, tk, lam_init):
    h = pl.program_id(1)
    j = pl.program_id(2)
    q = q_ref[...]
    lane = lax.broadcasted_iota(jnp.int32, q.shape, 1)
    zero = jnp.zeros_like(q)
    q1 = jnp.where(lane < DIFF_HALF, q, zero)
    q2 = jnp.where(lane >= DIFF_HALF, q, zero)
    slope = sl_ref[h]
    d0 = (lax.broadcasted_iota(jnp.int32, (tk, tq), 0)
          - lax.broadcasted_iota(jnp.int32, (tk, tq), 1))
    acc1[...] = jnp.zeros_like(acc1)
    acc2[...] = jnp.zeros_like(acc2)

    def online(s, m, l, acc, vt):
        m_new = jnp.maximum(m, jnp.max(s, axis=0, keepdims=True))
        a = jnp.exp2(m - m_new)
        p = jnp.exp2(s - m_new)
        l_new = a * l + jnp.sum(p, axis=0, keepdims=True)
        acc[...] = a * acc[...] + jnp.dot(vt, p.astype(BF16), preferred_element_type=F32)
        return m_new, l_new

    def body(kt, carry):
        m1, l1, m2, l2 = carry
        k = k_ref[pl.ds(pl.multiple_of(kt * tk, tk), tk), :]
        vt = vt_ref[kt]
        bias = slope * jnp.abs(d0 + (kt * tk - j * tq)).astype(F32)
        s1 = lax.dot_general(k, q1, _NT, preferred_element_type=F32) - bias
        s2 = lax.dot_general(k, q2, _NT, preferred_element_type=F32) - bias
        m1, l1 = online(s1, m1, l1, acc1, vt)
        m2, l2 = online(s2, m2, l2, acc2, vt)
        return m1, l1, m2, l2

    neg = jnp.full((1, tq), NEG_BIG, F32)
    zer = jnp.zeros((1, tq), F32)
    _, l1, _, l2 = lax.fori_loop(0, s_len // tk, body, (neg, zer, neg, zer))

    lv = lam_ref[...]
    lam = (jnp.exp(jnp.sum(lv[0:1] * lv[1:2], axis=1, keepdims=True))
           - jnp.exp(jnp.sum(lv[2:3] * lv[3:4], axis=1, keepdims=True)) + lam_init)
    o = acc1[...] / l1 - lam * (acc2[...] / l2)
    ms = jnp.mean(o * o, axis=0, keepdims=True)
    o = o * lax.rsqrt(ms + RMS_EPS) * (1.0 - lam_init)
    o = o * g_ref[...]
    o_ref[...] = o.T.astype(BF16)


def _diff_attn(proj, vt_all, slopes2, lam_l, g_col, *, row0, batch, s_len, tq, tk, lam_init):
    nq = s_len // tq
    qb0 = row0 // tq
    kb0 = row0 // s_len
    n_kh = DQ // HEAD_DIM
    return pl.pallas_call(
        functools.partial(_diff_attn_kernel, s_len=s_len, tq=tq, tk=tk, lam_init=lam_init),
        out_shape=jax.ShapeDtypeStruct((batch * s_len, DQ), BF16),
        grid=(batch, N_DIFF_HEADS, nq),
        in_specs=[
            pl.BlockSpec(memory_space=pltpu.SMEM),
            pl.BlockSpec((tq, HEAD_DIM), lambda b, h, j: (qb0 + b * nq + j, h)),
            pl.BlockSpec((s_len, HEAD_DIM), lambda b, h, j: (kb0 + b, n_kh + h)),
            pl.BlockSpec((s_len // tk, HEAD_DIM, tk), lambda b, h, j: (kb0 + b, h, 0)),
            pl.BlockSpec((4, DIFF_HALF), lambda b, h, j: (0, 0)),
            pl.BlockSpec((HEAD_DIM, 1), lambda b, h, j: (0, 0)),
        ],
        out_specs=pl.BlockSpec((tq, HEAD_DIM), lambda b, h, j: (b * nq + j, h)),
        scratch_shapes=[pltpu.VMEM((HEAD_DIM, tq), F32), pltpu.VMEM((HEAD_DIM, tq), F32)],
        compiler_params=_cparams(3),
        name="diff_attn",
    )(slopes2, proj, proj, vt_all, lam_l, g_col)


def _win_attn_kernel(sl_ref, sink_ref, q_ref, k_ref, v_ref, o_ref, *, s_len, tq):
    kv = pl.program_id(1)
    j = pl.program_id(2)
    w = tq + 2 * WINDOW
    ws = pl.multiple_of(jnp.clip(j * tq - WINDOW, 0, s_len - w), WINDOW)
    k = k_ref[pl.ds(ws, w), :]
    v = v_ref[pl.ds(ws, w), :]
    qpos = j * tq + lax.broadcasted_iota(jnp.int32, (tq, w), 0)
    kpos = ws + lax.broadcasted_iota(jnp.int32, (tq, w), 1)
    rel = jnp.abs(qpos - kpos)
    valid = rel <= WINDOW
    relf = rel.astype(F32)
    for g in range(WIN_GROUP):
        hidx = kv * WIN_GROUP + g
        qg = q_ref[:, g * HEAD_DIM:(g + 1) * HEAD_DIM]
        s = lax.dot_general(qg, k, _NT, preferred_element_type=F32)
        s = jnp.where(valid, s - sl_ref[hidx] * relf, NEG_BIG)
        sk = sink_ref[hidx]
        m = jnp.maximum(jnp.max(s, axis=1, keepdims=True), sk)
        p = jnp.exp2(s - m)
        den = jnp.sum(p, axis=1, keepdims=True) + jnp.exp2(sk - m)
        o = jnp.dot(p.astype(BF16), v, preferred_element_type=F32) / den
        o_ref[:, g * HEAD_DIM:(g + 1) * HEAD_DIM] = o.astype(BF16)


def _win_attn(proj, slopes2, sink2, *, row0, batch, s_len, tq):
    nq = s_len // tq
    qb0 = row0 // tq
    kb0 = row0 // s_len
    gw = WIN_GROUP * HEAD_DIM
    q_col0 = (2 * DQ) // gw
    k_col0 = (2 * DQ + WQ) // HEAD_DIM
    v_col0 = (2 * DQ + WQ + WKV) // HEAD_DIM
    return pl.pallas_call(
        functools.partial(_win_attn_kernel, s_len=s_len, tq=tq),
        out_shape=jax.ShapeDtypeStruct((batch * s_len, WQ), BF16),
        grid=(batch, N_WIN_KV, nq),
        in_specs=[
            pl.BlockSpec(memory_space=pltpu.SMEM),
            pl.BlockSpec(memory_space=pltpu.SMEM),
            pl.BlockSpec((tq, gw), lambda b, kv, j: (qb0 + b * nq + j, q_col0 + kv)),
            pl.BlockSpec((s_len, HEAD_DIM), lambda b, kv, j: (kb0 + b, k_col0 + kv)),
            pl.BlockSpec((s_len, HEAD_DIM), lambda b, kv, j: (kb0 + b, v_col0 + kv)),
        ],
        out_specs=pl.BlockSpec((tq, gw), lambda b, kv, j: (b * nq + j, kv)),
        compiler_params=_cparams(3),
        name="win_attn",
    )(slopes2, sink2, proj, proj, proj)


def _pack_rows(v):
    half = v.shape[1] // 2
    lo = lax.bitcast_convert_type(v[:, :half].astype(BF16).astype(F32), jnp.uint32)
    hi = lax.bitcast_convert_type(v[:, half:].astype(BF16).astype(F32), jnp.uint32)
    return (lo >> 16) | hi


def _unpack_rows(u):
    lo = lax.bitcast_convert_type(u << 16, F32)
    hi = lax.bitcast_convert_type(u & jnp.uint32(0xFFFF0000), F32)
    return lo, hi


def _store_token_major(ref, u):
    rows, width = u.shape
    p = width // LANES
    for c in range(p):
        ref[pl.ds(c, rows, stride=p), :] = u[:, c * LANES:(c + 1) * LANES]


def _load_token_major(ref, rows):
    p = ref.shape[0] // rows
    return jnp.concatenate([ref[pl.ds(c, rows, stride=p), :] for c in range(p)], axis=1)


def _layer_norm(z, g, b):
    mu = jnp.mean(z, axis=1, keepdims=True)
    zc = z - mu
    var = jnp.mean(zc * zc, axis=1, keepdims=True)
    return zc * lax.rsqrt(var + LN_EPS) * g + b


def _outproj_kernel(hda_ref, hdb_ref, hwa_ref, hwb_ref, xa_ref, xb_ref, wod_ref, wow_ref,
                    g1_ref, lng_ref, lnb_ref, sc_ref, sh_ref, wrh_ref, wrl_ref, br_ref,
                    x1_ref, h2_ref, er_ref, w0_ref, w1_ref, cnt_ref, *, npb, alpha, tm):
    i = pl.program_id(0)
    x = _select_x(i, npb, xa_ref, xb_ref)
    hd = _select_x(i, npb, hda_ref, hdb_ref)
    hw = _select_x(i, npb, hwa_ref, hwb_ref)
    att = (jnp.dot(hd, wod_ref[...], preferred_element_type=F32)
           + jnp.dot(hw, wow_ref[...], preferred_element_type=F32))
    x1 = _layer_norm(alpha * x + g1_ref[...] * att, lng_ref[...], lnb_ref[...])
    x1_ref[...] = x1
    h2 = x1 * (1.0 + sc_ref[...]) + sh_ref[...]
    _store_token_major(h2_ref, _pack_rows(h2))

    h_hi = h2.astype(BF16)
    h_lo = (h2 - h_hi.astype(F32)).astype(BF16)
    lt = (jnp.dot(h_hi, wrh_ref[...], preferred_element_type=F32)
          + jnp.dot(h_lo, wrh_ref[...], preferred_element_type=F32)
          + jnp.dot(h_hi, wrl_ref[...], preferred_element_type=F32))
    lt = lt.T + br_ref[...]
    iota8 = lax.broadcasted_iota(jnp.int32, (SUBLANES, tm), 0)

    def first_argmax(v):
        vmax = jnp.max(v, axis=0, keepdims=True)
        idx = jnp.min(jnp.where(v == vmax, iota8, SUBLANES), axis=0, keepdims=True)
        return vmax, idx

    gl = lt[0:SUBLANES]
    gmax, g_idx = first_argmax(gl)
    g_w = 1.0 / jnp.sum(jnp.exp(gl - gmax), axis=0, keepdims=True)
    el = jnp.zeros((SUBLANES, tm), F32)
    for g in range(N_GROUPS):
        lo = SUBLANES + g * EXPERTS_PER_GROUP
        el = jnp.where(g_idx == g, lt[lo:lo + EXPERTS_PER_GROUP], el)
    v0, i0 = first_argmax(el)
    el2 = jnp.where(iota8 == i0, -jnp.inf, el)
    v1, i1 = first_argmax(el2)
    t = jnp.exp(v1 - v0)
    w0 = g_w / (1.0 + t)
    w1 = g_w * t / (1.0 + t)
    e0 = g_idx * EXPERTS_PER_GROUP + i0
    e1 = g_idx * EXPERTS_PER_GROUP + i1

    @pl.when(i == 0)
    def _():
        cnt_ref[...] = jnp.zeros_like(cnt_ref)

    iota_e = lax.broadcasted_iota(jnp.int32, (N_EXPERTS, tm), 0)
    hit0 = iota_e == e0
    hit1 = iota_e == e1
    c = jnp.where(hit0, 1.0, jnp.where(hit1, 1.0, 0.0))
    upper = (lax.broadcasted_iota(jnp.int32, (tm, tm), 0)
             < lax.broadcasted_iota(jnp.int32, (tm, tm), 1))
    before = jnp.dot(c.astype(BF16), jnp.where(upper, 1.0, 0.0).astype(BF16),
                     preferred_element_type=F32)
    tot = before + cnt_ref[:, 0:1]
    r0 = jnp.sum(jnp.where(hit0, tot, 0.0), axis=0, keepdims=True).astype(jnp.int32)
    r1 = jnp.sum(jnp.where(hit1, tot, 0.0), axis=0, keepdims=True).astype(jnp.int32)
    cnt_ref[...] = cnt_ref[...] + jnp.sum(c, axis=1, keepdims=True)

    er_ref[...] = jnp.where(iota8 == 0, e0, jnp.where(iota8 == 1, e1,
                            jnp.where(iota8 == 2, r0, jnp.where(iota8 == 3, r1, 0))))
    w0_ref[...] = jnp.broadcast_to(w0, (LANES, tm)).T
    w1_ref[...] = jnp.broadcast_to(w1, (LANES, tm)).T


def _outproj(lay, hd, hw, xa, xb, same_array, mod3, wo_d, wo_w, ln_g, ln_b, wr_hi, wr_lo, br, alpha, tm):
    d = xa.shape[1]
    p = d // (2 * LANES)
    const = dict(pipeline_mode=pl.Buffered(1))
    vec = lambda: pl.BlockSpec((1, d), lambda i: (0, 0))
    return pl.pallas_call(
        functools.partial(_outproj_kernel, npb=lay.tp // tm, alpha=alpha, tm=tm),
        out_shape=(jax.ShapeDtypeStruct((lay.t, d), F32),
                   jax.ShapeDtypeStruct((lay.t * p, LANES), jnp.uint32),
                   jax.ShapeDtypeStruct((SUBLANES, lay.t), jnp.int32),
                   jax.ShapeDtypeStruct((lay.t, LANES), F32),
                   jax.ShapeDtypeStruct((lay.t, LANES), F32),
                   jax.ShapeDtypeStruct((N_EXPERTS, LANES), F32)),
        grid=(lay.t // tm,),
        in_specs=_x_specs(lay, tm, DQ, False) + _x_specs(lay, tm, WQ, False)
        + _x_specs(lay, tm, d, same_array) + [
            pl.BlockSpec((DQ, d), lambda i: (0, 0), **const),
            pl.BlockSpec((WQ, d), lambda i: (0, 0), **const),
            _mod_spec(lay, tm, 2, d),
            vec(), vec(),
            _mod_spec(lay, tm, 4, d),
            _mod_spec(lay, tm, 3, d),
            pl.BlockSpec((d, LANES), lambda i: (0, 0)),
            pl.BlockSpec((d, LANES), lambda i: (0, 0)),
            pl.BlockSpec((LANES, 1), lambda i: (0, 0)),
        ],
        out_specs=(pl.BlockSpec((tm, d), lambda i: (i, 0)),
                   pl.BlockSpec((tm * p, LANES), lambda i: (i, 0)),
                   pl.BlockSpec((SUBLANES, tm), lambda i: (0, i)),
                   pl.BlockSpec((tm, LANES), lambda i: (i, 0)),
                   pl.BlockSpec((tm, LANES), lambda i: (i, 0)),
                   pl.BlockSpec((N_EXPERTS, LANES), lambda i: (0, 0))),
        compiler_params=_cparams(1),
        name="out_proj_ln_router",
    )(*hd, *hw, xa, xb, wo_d, wo_w, mod3, ln_g, ln_b, mod3, mod3, wr_hi, wr_lo, br)


def _token_copy(src, s_tok, dst, d_tok, sem, p):
    return pltpu.make_async_copy(src.at[pl.ds(pl.multiple_of(s_tok * p, p), p)],
                                 dst.at[pl.ds(pl.multiple_of(d_tok * p, p), p)], sem)


def _dispatch_kernel(zrow_ref, pos_ref, h_ref, xs_ref, zbuf, sem, zsem, *, ct, tm, p):
    i = pl.program_id(0)

    def zero_copy(row):
        start = pl.multiple_of(row * p, tm * p)
        return pltpu.make_async_copy(zbuf, xs_ref.at[pl.ds(start, tm * p)], zsem)

    @pl.when(i == 0)
    def _():
        zbuf[...] = jnp.zeros_like(zbuf)
        n_tiles = xs_ref.shape[0] // (tm * p)
        for e in range(N_EXPERTS):
            @pl.when(zrow_ref[e] >= 0)
            def _():
                zero_copy(zrow_ref[e]).start()
        lax.fori_loop(zrow_ref[N_EXPERTS], n_tiles, lambda k, c: (zero_copy(k * tm).start(), c)[1], 0)
        for e in range(N_EXPERTS):
            @pl.when(zrow_ref[e] >= 0)
            def _():
                zero_copy(zrow_ref[e]).wait()
        lax.fori_loop(zrow_ref[N_EXPERTS], n_tiles, lambda k, c: (zero_copy(k * tm).wait(), c)[1], 0)

    def body(j, carry):
        t = i * ct + j
        _token_copy(h_ref, t, xs_ref, pos_ref[2 * j], sem, p).start()
        _token_copy(h_ref, t, xs_ref, pos_ref[2 * j + 1], sem, p).start()
        return carry

    lax.fori_loop(0, ct, body, 0, unroll=8)
    pltpu.make_async_copy(h_ref.at[pl.ds(0, 2 * ct * p)], xs_ref.at[pl.ds(0, 2 * ct * p)], sem).wait()


def _dispatch(zrow, pos_flat, h2, n_rows, ct, tm, p):
    t = h2.shape[0] // p
    return pl.pallas_call(
        functools.partial(_dispatch_kernel, ct=ct, tm=tm, p=p),
        out_shape=jax.ShapeDtypeStruct((n_rows * p, LANES), jnp.uint32),
        grid=(t // ct,),
        in_specs=[
            pl.BlockSpec(memory_space=pltpu.SMEM),
            pl.BlockSpec((2 * ct,), lambda i: (i,), memory_space=pltpu.SMEM),
            pl.BlockSpec(memory_space=pl.ANY),
        ],
        out_specs=pl.BlockSpec(memory_space=pl.ANY),
        scratch_shapes=[pltpu.VMEM((tm * p, LANES), jnp.uint32),
                        pltpu.SemaphoreType.DMA, pltpu.SemaphoreType.DMA],
        compiler_params=_cparams(1),
        name="moe_dispatch",
    )(zrow, pos_flat, h2)


def _combine_kernel(pos_ref, ys_ref, ya_ref, yb_ref, sem, *, ct, p):
    i = pl.program_id(0)

    def body(j, carry):
        t = i * ct + j
        _token_copy(ys_ref, pos_ref[2 * j], ya_ref, t, sem, p).start()
        _token_copy(ys_ref, pos_ref[2 * j + 1], yb_ref, t, sem, p).start()
        return carry

    lax.fori_loop(0, ct, body, 0, unroll=8)
    pltpu.make_async_copy(ys_ref.at[pl.ds(0, 2 * ct * p)], ya_ref.at[pl.ds(0, 2 * ct * p)], sem).wait()


def _combine(pos_flat, ys, t, ct, p):
    shape = jax.ShapeDtypeStruct((t * p, LANES), jnp.uint32)
    return pl.pallas_call(
        functools.partial(_combine_kernel, ct=ct, p=p),
        out_shape=(shape, shape),
        grid=(t // ct,),
        in_specs=[
            pl.BlockSpec((2 * ct,), lambda i: (i,), memory_space=pltpu.SMEM),
            pl.BlockSpec(memory_space=pl.ANY),
        ],
        out_specs=(pl.BlockSpec(memory_space=pl.ANY), pl.BlockSpec(memory_space=pl.ANY)),
        scratch_shapes=[pltpu.SemaphoreType.DMA],
        compiler_params=_cparams(1),
        name="moe_combine",
    )(pos_flat, ys)


def _experts_kernel(te_ref, nu_ref, x_ref, wg_ref, wu_ref, wd_ref, o_ref, *, tm):
    @pl.when(pl.program_id(0) < nu_ref[0])
    def _():
        lo, hi = _unpack_rows(_load_token_major(x_ref, tm))
        x = jnp.concatenate([lo.astype(BF16), hi.astype(BF16)], axis=1)
        g = jnp.dot(x, wg_ref[...], preferred_element_type=F32)
        u = jnp.dot(x, wu_ref[...], preferred_element_type=F32)
        a = (g * jax.nn.sigmoid(g) * u).astype(BF16)
        y = jnp.dot(a, wd_ref[...], preferred_element_type=F32)
        _store_token_major(o_ref, _pack_rows(y))

    @pl.when(pl.program_id(0) >= nu_ref[0])
    def _():
        o_ref[...] = jnp.zeros_like(o_ref)


def _experts(tile_expert, n_used, xs, wg, wu, wd, tm, p):
    d, f = wg.shape[1], wg.shape[2]
    n_rows = xs.shape[0] // p
    n_tiles = n_rows // tm

    def tile(i, te, nu):
        return jnp.minimum(i, nu[0] - 1)

    grid_spec = pltpu.PrefetchScalarGridSpec(
        num_scalar_prefetch=2,
        grid=(n_tiles,),
        in_specs=[
            pl.BlockSpec((tm * p, LANES), lambda i, te, nu: (tile(i, te, nu), 0)),
            pl.BlockSpec((None, d, f), lambda i, te, nu: (te[tile(i, te, nu)], 0, 0)),
            pl.BlockSpec((None, d, f), lambda i, te, nu: (te[tile(i, te, nu)], 0, 0)),
            pl.BlockSpec((None, f, d), lambda i, te, nu: (te[tile(i, te, nu)], 0, 0)),
        ],
        out_specs=pl.BlockSpec((tm * p, LANES), lambda i, te, nu: (i, 0)),
    )
    return pl.pallas_call(
        functools.partial(_experts_kernel, tm=tm),
        out_shape=jax.ShapeDtypeStruct((n_rows * p, LANES), jnp.uint32),
        grid_spec=grid_spec,
        compiler_params=_cparams(1),
        name="moe_experts",
    )(tile_expert, n_used, xs, wg, wu, wd)


def _final_kernel(x1_ref, ya_ref, yb_ref, w0_ref, w1_ref, g2_ref, lng_ref, lnb_ref, o_ref, *, alpha, tm):
    reps = x1_ref.shape[1] // LANES
    w0 = jnp.concatenate([w0_ref[...]] * reps, axis=1)
    w1 = jnp.concatenate([w1_ref[...]] * reps, axis=1)
    ya = jnp.concatenate(_unpack_rows(_load_token_major(ya_ref, tm)), axis=1)
    yb = jnp.concatenate(_unpack_rows(_load_token_major(yb_ref, tm)), axis=1)
    y = w0 * ya + w1 * yb
    o_ref[...] = _layer_norm(alpha * x1_ref[...] + g2_ref[...] * y, lng_ref[...], lnb_ref[...])


def _final(lay, x1, ya, yb, w0, w1, mod3, ln_g, ln_b, alpha, tm, p, row0, n_rows):
    d = x1.shape[1]
    b0 = row0 // tm
    row = lambda i: (b0 + i, 0)
    vec = lambda: pl.BlockSpec((1, d), lambda i: (0, 0))
    return pl.pallas_call(
        functools.partial(_final_kernel, alpha=alpha, tm=tm),
        out_shape=jax.ShapeDtypeStruct((n_rows, d), F32),
        grid=(n_rows // tm,),
        in_specs=[
            pl.BlockSpec((tm, d), row), pl.BlockSpec((tm * p, LANES), row), pl.BlockSpec((tm * p, LANES), row),
            pl.BlockSpec((tm, LANES), row), pl.BlockSpec((tm, LANES), row),
            pl.BlockSpec((None, 1, d), lambda i: (lay.batch_of_tile(b0 + i, tm), 0, 5)),
            vec(), vec(),
        ],
        out_specs=pl.BlockSpec((tm, d), lambda i: (i, 0)),
        compiler_params=_cparams(1),
        name="moe_combine_ln",
    )(x1, ya, yb, w0, w1, mod3, ln_g, ln_b)


def _routing_tables(er, cnt, tm, n_tiles):
    counts = cnt[:, 0].astype(jnp.int32)
    padded = ((counts + tm - 1) // tm) * tm
    ends = jnp.cumsum(padded)
    offs = ends - padded
    pos0 = offs[er[0]] + er[2]
    pos1 = offs[er[1]] + er[3]
    pos_flat = jnp.stack([pos0, pos1], axis=1).reshape(-1)
    tile_start = jnp.arange(n_tiles, dtype=jnp.int32) * tm
    tile_expert = jnp.minimum(
        jnp.sum((tile_start[:, None] >= ends[None, :]).astype(jnp.int32), axis=1), N_EXPERTS - 1)
    n_used = (ends[-1] // tm).reshape(1)
    zrow = jnp.concatenate([jnp.where(counts > 0, ends - tm, -1), n_used])
    return pos_flat, tile_expert.astype(jnp.int32), n_used.astype(jnp.int32), zrow.astype(jnp.int32)


def kernel(x_prompt, x_sample, c_prompt, c_sample, w_ada, b_ada, w_in, lam, subln_g, sink,
           w_o, ln_g, ln_b, w_rg, b_rg, w_re, b_re, w_gate, w_up, w_down):
    bp, sp, d = x_prompt.shape
    bs, ss, _ = x_sample.shape
    depth = w_ada.shape[0]
    lay = _Layout(bp, sp, bs, ss)
    alpha = (2.0 * depth) ** 0.25
    assert d % (2 * LANES * SUBLANES) == 0 and lay.tp % ss == 0
    p = d // (2 * LANES)

    tm = _pick(math.gcd(sp, ss), 512)
    tq_d = tm
    tk_d = tm
    tq_w = _pick(math.gcd(sp, ss), 512)
    while tq_w + 2 * WINDOW > min(sp, ss):
        tq_w //= 2
    ct = _pick(lay.t, 1024)
    n_tiles = (2 * lay.t + N_EXPERTS * (tm - 1) + tm - 1) // tm
    n_rows = n_tiles * tm

    xa = x_prompt.reshape(lay.tp, d)
    xb = x_sample.reshape(bs * ss, d)

    nb = bp + bs
    c_pad = jnp.zeros((-(-nb // SUBLANES) * SUBLANES, d), F32)
    c_pad = c_pad.at[:bp].set(c_prompt).at[bp:nb].set(c_sample)
    mod = _adaln(c_pad, w_ada, b_ada)

    slopes_d = jnp.asarray([LOG2E * 2.0 ** (-8.0 * (h + 1) / N_DIFF_HEADS) for h in range(N_DIFF_HEADS)], F32)
    slopes_w = jnp.asarray([LOG2E * 2.0 ** (-8.0 * (h + 1) / N_WIN_HEADS) for h in range(N_WIN_HEADS)], F32)
    cs = jnp.ones((2 * DQ + WQ + 2 * WKV,), F32)
    cs = cs.at[:DQ].set(LOG2E * DIFF_HALF ** -0.5).at[2 * DQ:2 * DQ + WQ].set(LOG2E * HEAD_DIM ** -0.5)
    cs = cs.reshape(1, -1)

    same = False
    x_last = None
    for l in range(depth):
        lam_init = 0.8 - 0.6 * math.exp(-0.3 * l)
        mod3 = mod[l].reshape(-1, 1, 6 * d)
        wl = w_in[l]
        w_main = jnp.concatenate([wl[:, :2 * DQ], wl[:, 3 * DQ:]], axis=1).astype(BF16)
        wvt = wl[:, 2 * DQ:3 * DQ].T.astype(BF16)
        proj, vt_all = _inproj(lay, xa, xb, same, mod3, w_main, wvt, cs, tm)

        g_col = subln_g[l].reshape(HEAD_DIM, 1)
        groups = (dict(row0=0, batch=bp, s_len=sp), dict(row0=lay.tp, batch=bs, s_len=ss))
        hd = [_diff_attn(proj, vt_all, slopes_d, lam[l], g_col, lam_init=lam_init, tq=tq_d, tk=tk_d, **g)
              for g in groups]
        sink2 = sink[l].astype(F32) * LOG2E
        hw = [_win_attn(proj, slopes_w, sink2, tq=tq_w, **g) for g in groups]

        wr = jnp.zeros((d, LANES), F32)
        wr = wr.at[:, :N_GROUPS].set(w_rg[l])
        wr = wr.at[:, SUBLANES:SUBLANES + N_EXPERTS].set(
            jnp.transpose(w_re[l], (1, 0, 2)).reshape(d, N_EXPERTS))
        wr_hi = wr.astype(BF16)
        wr_lo = (wr - wr_hi.astype(F32)).astype(BF16)
        br = jnp.full((LANES,), NEG_BIG, F32)
        br = br.at[:N_GROUPS].set(b_rg[l]).at[SUBLANES:SUBLANES + N_EXPERTS].set(b_re[l].reshape(-1))
        br = br.reshape(LANES, 1)

        wo = w_o[l].astype(BF16)
        x1, h2, er, w0, w1, cnt = _outproj(
            lay, hd, hw, xa, xb, same, mod3, wo[:DQ], wo[DQ:], ln_g[l, 0].reshape(1, d),
            ln_b[l, 0].reshape(1, d), wr_hi, wr_lo, br, alpha, tm)

        pos_flat, tile_expert, n_used, zrow = _routing_tables(er, cnt, tm, n_tiles)
        xs = _dispatch(zrow, pos_flat, h2, n_rows, ct, tm, p)
        ys = _experts(tile_expert, n_used, xs, w_gate[l].astype(BF16), w_up[l].astype(BF16),
                      w_down[l].astype(BF16), tm, p)
        ya, yb = _combine(pos_flat, ys, lay.t, ct, p)

        fin = functools.partial(_final, lay, x1, ya, yb, w0, w1, mod3, ln_g[l, 1].reshape(1, d),
                                ln_b[l, 1].reshape(1, d), alpha, tm, p)
        if l + 1 < depth:
            x_all = fin(0, lay.t)
            xa = xb = x_all
            same = True
        else:
            x_last = (fin(0, lay.tp), fin(lay.tp, lay.t - lay.tp))

    return (x_last[0].reshape(bp, sp, d), x_last[1].reshape(bs, ss, d))
```

```python
import functools
import math

import jax
import jax.numpy as jnp
from jax import lax
from jax.experimental import pallas as pl
from jax.experimental.pallas import tpu as pltpu

F32 = jnp.float32
BF16 = jnp.bfloat16

HEAD_DIM = 128
DIFF_HALF = HEAD_DIM // 2
N_DIFF_HEADS = 8
N_WIN_HEADS = 8
N_WIN_KV = 2
WIN_GROUP = N_WIN_HEADS // N_WIN_KV
WINDOW = 128
DQ = N_DIFF_HEADS * HEAD_DIM
WQ = N_WIN_HEADS * HEAD_DIM
WKV = N_WIN_KV * HEAD_DIM
N_GROUPS = 4
EXPERTS_PER_GROUP = 8
N_EXPERTS = N_GROUPS * EXPERTS_PER_GROUP
LN_EPS = 1e-5
RMS_EPS = 1e-5
LOG2E = 1.4426950408889634
NEG_BIG = -1e30

V7X_VMEM_BYTES = 64 * 1024 * 1024
VMEM_LIMIT = V7X_VMEM_BYTES - 8 * 1024 * 1024
LANES = 128
SUBLANES = 8

_NT = (((1,), (1,)), ((), ()))


def _cparams(n_axes):
    return pltpu.CompilerParams(
        dimension_semantics=("arbitrary",) * n_axes, vmem_limit_bytes=VMEM_LIMIT)


def _pick(n, pref):
    t = min(pref, n)
    while n % t:
        t //= 2
    return t


def _adaln_kernel(c_ref, w_ref, b_ref, o_ref):
    c = c_ref[...]
    s = (c * jax.nn.sigmoid(c)).astype(BF16)
    o_ref[...] = jnp.dot(s, w_ref[...].astype(BF16), preferred_element_type=F32) + b_ref[...]


def _adaln(c_pad, w_ada, b_ada):
    depth, d, n = w_ada.shape
    tn = _pick(n, 1024)
    rows = c_pad.shape[0]
    return pl.pallas_call(
        _adaln_kernel,
        out_shape=jax.ShapeDtypeStruct((depth, rows, n), F32),
        grid=(depth, n // tn),
        in_specs=[
            pl.BlockSpec((rows, d), lambda l, j: (0, 0)),
            pl.BlockSpec((None, d, tn), lambda l, j: (l, 0, j)),
            pl.BlockSpec((None, 1, tn), lambda l, j: (l, 0, j)),
        ],
        out_specs=pl.BlockSpec((None, rows, tn), lambda l, j: (l, 0, j)),
        compiler_params=_cparams(2),
        name="adaln_mod",
    )(c_pad, w_ada, b_ada.reshape(depth, 1, n))


class _Layout:
    def __init__(self, bp, sp, bs, ss):
        self.bp, self.sp, self.bs, self.ss = bp, sp, bs, ss
        self.tp = bp * sp
        self.t = self.tp + bs * ss

    def batch_of_tile(self, i, tm):
        row = i * tm
        return jnp.where(row < self.tp, row // self.sp, self.bp + (row - self.tp) // self.ss)


def _mod_spec(lay, tm, chunk, d):
    return pl.BlockSpec((None, 1, d), lambda i: (lay.batch_of_tile(i, tm), 0, chunk))


def _x_specs(lay, tm, d, same_array):
    npb = lay.tp // tm
    boff = npb if same_array else 0
    return [
        pl.BlockSpec((tm, d), lambda i: (jnp.minimum(i, npb - 1), 0)),
        pl.BlockSpec((tm, d), lambda i: (jnp.maximum(i - npb, 0) + boff, 0)),
    ]


def _select_x(i, npb, xa_ref, xb_ref):
    return jnp.where(i < npb, xa_ref[...], xb_ref[...])


def _inproj_kernel(xa_ref, xb_ref, sc_ref, sh_ref, w_ref, wvt_ref, cs_ref, proj_ref, vt_ref, *, npb):
    x = _select_x(pl.program_id(0), npb, xa_ref, xb_ref)
    h = (x * (1.0 + sc_ref[...]) + sh_ref[...]).astype(BF16)
    acc = jnp.dot(h, w_ref[...], preferred_element_type=F32)
    proj_ref[...] = (acc * cs_ref[...]).astype(BF16)
    vt = lax.dot_general(wvt_ref[...], h, _NT, preferred_element_type=F32)
    vt_ref[...] = vt.astype(BF16)


def _inproj(lay, xa, xb, same_array, mod3, w_main, wvt, col_scale, tm):
    d = xa.shape[1]
    n = w_main.shape[1]
    nv = wvt.shape[0]
    const = dict(pipeline_mode=pl.Buffered(1))
    return pl.pallas_call(
        functools.partial(_inproj_kernel, npb=lay.tp // tm),
        out_shape=(jax.ShapeDtypeStruct((lay.t, n), BF16),
                   jax.ShapeDtypeStruct((lay.t // tm, nv, tm), BF16)),
        grid=(lay.t // tm,),
        in_specs=_x_specs(lay, tm, d, same_array) + [
            _mod_spec(lay, tm, 1, d),
            _mod_spec(lay, tm, 0, d),
            pl.BlockSpec((d, n), lambda i: (0, 0), **const),
            pl.BlockSpec((nv, d), lambda i: (0, 0), **const),
            pl.BlockSpec((1, n), lambda i: (0, 0)),
        ],
        out_specs=(pl.BlockSpec((tm, n), lambda i: (i, 0)),
                   pl.BlockSpec((None, nv, tm), lambda i: (i, 0, 0))),
        compiler_params=_cparams(1),
        name="in_proj",
    )(xa, xb, mod3, mod3, w_main, wvt, col_scale)


def _diff_attn_kernel(sl_ref, tab_ref, q_ref, k_ref, kaug_ref, vt_ref, lam_ref, g_ref, o_ref,
                      acc1, acc2, sa1, sa2, sb1, sb2, *, s_len, tq, tk, lam_init):
    nk = s_len // tk
    n_off = nk - 1
    h = pl.program_id(1)
    j = pl.program_id(2)
    q = q_ref[...]
    lane = lax.broadcasted_iota(jnp.int32, q.shape, 1)
    zero = jnp.zeros_like(q)
    q1 = jnp.where(lane < DIFF_HALF, q, zero)
    q2 = jnp.where(lane >= DIFF_HALF, q, zero)
    slope = sl_ref[h]
    aug = jnp.broadcast_to(tab_ref[pl.ds(h, 1), :], q.shape).astype(BF16)
    qc1 = jnp.concatenate([q1, aug], axis=1)
    qc2 = jnp.concatenate([q2, aug], axis=1)
    kaug = kaug_ref[...]
    kaug_neg = -kaug
    rq = lax.broadcasted_iota(jnp.int32, (1, tq), 1).astype(F32)
    acc1[...] = jnp.zeros_like(acc1)
    acc2[...] = jnp.zeros_like(acc2)

    def online(s, mx, cq, m, l, acc, vt):
        m_new = jnp.maximum(m, mx + cq)
        a = jnp.exp2(m - m_new)
        p = jnp.exp2(s - (m_new - cq))
        l_new = a * l + jnp.sum(p, axis=0, keepdims=True)
        acc[...] = a * acc[...] + jnp.dot(vt, p.astype(BF16), preferred_element_type=F32)
        return m_new, l_new

    def scores(t, buf1, buf2):
        tc = jnp.minimum(t, n_off - 1)
        kt = tc + (tc >= j).astype(jnp.int32)
        before = kt < j
        k = k_ref[pl.ds(pl.multiple_of(kt * tk, tk), tk), :]
        kc = jnp.concatenate([k, jnp.where(before, kaug, kaug_neg)], axis=1)
        s1 = lax.dot_general(kc, qc1, _NT, preferred_element_type=F32)
        s2 = lax.dot_general(kc, qc2, _NT, preferred_element_type=F32)
        buf1[...] = s1
        buf2[...] = s2
        cq = jnp.where(before, -slope, slope) * (rq + (j * tq - kt * tk).astype(F32))
        cq = jnp.where(t < n_off, cq, NEG_BIG)
        return kt, cq, jnp.max(s1, axis=0, keepdims=True), jnp.max(s2, axis=0, keepdims=True)

    def consume(tile, buf1, buf2, state):
        kt, cq, mx1, mx2 = tile
        m1, l1, m2, l2 = state
        vt = vt_ref[kt]
        m1, l1 = online(buf1[...], mx1, cq, m1, l1, acc1, vt)
        m2, l2 = online(buf2[...], mx2, cq, m2, l2, acc2, vt)
        return m1, l1, m2, l2

    k = k_ref[pl.ds(pl.multiple_of(j * tk, tk), tk), :]
    dist = jnp.abs(lax.broadcasted_iota(jnp.int32, (tk, tq), 0)
                   - lax.broadcasted_iota(jnp.int32, (tk, tq), 1)).astype(F32)
    bias = slope * dist
    s1 = lax.dot_general(k, q1, _NT, preferred_element_type=F32) - bias
    s2 = lax.dot_general(k, q2, _NT, preferred_element_type=F32) - bias
    neg = jnp.full((1, tq), NEG_BIG, F32)
    zer = jnp.zeros((1, tq), F32)
    vt = vt_ref[j]
    m1, l1 = online(s1, jnp.max(s1, axis=0, keepdims=True), zer, neg, zer, acc1, vt)
    m2, l2 = online(s2, jnp.max(s2, axis=0, keepdims=True), zer, neg, zer, acc2, vt)
    state = (m1, l1, m2, l2)

    if n_off > 0:
        def body(u, carry):
            state, tile_a = carry
            tile_b = scores(2 * u + 1, sb1, sb2)
            state = consume(tile_a, sa1, sa2, state)
            tile_a = scores(2 * u + 2, sa1, sa2)
            state = consume(tile_b, sb1, sb2, state)
            return state, tile_a

        state, _ = lax.fori_loop(0, (n_off + 1) // 2, body, (state, scores(0, sa1, sa2)))
    _, l1, _, l2 = state

    lv = lam_ref[...]
    lam = (jnp.exp(jnp.sum(lv[0:1] * lv[1:2], axis=1, keepdims=True))
           - jnp.exp(jnp.sum(lv[2:3] * lv[3:4], axis=1, keepdims=True)) + lam_init)
    o = acc1[...] / l1 - lam * (acc2[...] / l2)
    ms = jnp.mean(o * o, axis=0, keepdims=True)
    o = o * lax.rsqrt(ms + RMS_EPS) * (1.0 - lam_init)
    o = o * g_ref[...]
    o_ref[...] = o.T.astype(BF16)


def _alibi_tables(slopes2, tk):
    assert tk <= 256 * 256
    r = jnp.arange(tk, dtype=jnp.int32)
    kaug = jnp.zeros((tk, LANES), F32)
    kaug = kaug.at[:, 0:3].set((r % 256).astype(F32)[:, None]).at[:, 3:6].set((r // 256).astype(F32)[:, None])
    s_a = slopes2.astype(BF16).astype(F32)
    s_b = (slopes2 - s_a).astype(BF16).astype(F32)
    s_c = (slopes2 - s_a - s_b).astype(BF16).astype(F32)
    pieces = jnp.stack([s_a, s_b, s_c], axis=1)
    tab = jnp.zeros((slopes2.shape[0], LANES), F32)
    tab = tab.at[:, 0:3].set(pieces).at[:, 3:6].set(256.0 * pieces)
    return tab, kaug.astype(BF16)


def _diff_attn(proj, vt_all, slopes2, tab, kaug, lam_l, g_col, *, row0, batch, s_len, tq, tk, lam_init):
    assert tq == tk
    nq = s_len // tq
    qb0 = row0 // tq
    kb0 = row0 // s_len
    n_kh = DQ // HEAD_DIM
    return pl.pallas_call(
        functools.partial(_diff_attn_kernel, s_len=s_len, tq=tq, tk=tk, lam_init=lam_init),
        out_shape=jax.ShapeDtypeStruct((batch * s_len, DQ), BF16),
        grid=(batch, N_DIFF_HEADS, nq),
        in_specs=[
            pl.BlockSpec(memory_space=pltpu.SMEM),
            pl.BlockSpec(tab.shape, lambda b, h, j: (0, 0)),
            pl.BlockSpec((tq, HEAD_DIM), lambda b, h, j: (qb0 + b * nq + j, h)),
            pl.BlockSpec((s_len, HEAD_DIM), lambda b, h, j: (kb0 + b, n_kh + h)),
            pl.BlockSpec((tk, LANES), lambda b, h, j: (0, 0)),
            pl.BlockSpec((s_len // tk, HEAD_DIM, tk), lambda b, h, j: (kb0 + b, h, 0)),
            pl.BlockSpec((4, DIFF_HALF), lambda b, h, j: (0, 0)),
            pl.BlockSpec((HEAD_DIM, 1), lambda b, h, j: (0, 0)),
        ],
        out_specs=pl.BlockSpec((tq, HEAD_DIM), lambda b, h, j: (b * nq + j, h)),
        scratch_shapes=[pltpu.VMEM((HEAD_DIM, tq), F32)] * 2 + [pltpu.VMEM((tk, tq), F32)] * 4,
        compiler_params=_cparams(3),
        name="diff_attn",
    )(slopes2, tab, proj, proj, kaug, vt_all, lam_l, g_col)


def _win_attn_kernel(sl_ref, sink_ref, q_ref, k_ref, v_ref, o_ref, *, s_len, tq):
    kv = pl.program_id(1)
    j = pl.program_id(2)
    w = tq + 2 * WINDOW
    ws = pl.multiple_of(jnp.clip(j * tq - WINDOW, 0, s_len - w), WINDOW)
    k = k_ref[pl.ds(ws, w), :]
    v = v_ref[pl.ds(ws, w), :]
    qpos = j * tq + lax.broadcasted_iota(jnp.int32, (tq, w), 0)
    kpos = ws + lax.broadcasted_iota(jnp.int32, (tq, w), 1)
    rel = jnp.abs(qpos - kpos)
    valid = rel <= WINDOW
    relf = rel.astype(F32)
    for g in range(WIN_GROUP):
        hidx = kv * WIN_GROUP + g
        qg = q_ref[:, g * HEAD_DIM:(g + 1) * HEAD_DIM]
        s = lax.dot_general(qg, k, _NT, preferred_element_type=F32)
        s = jnp.where(valid, s - sl_ref[hidx] * relf, NEG_BIG)
        sk = sink_ref[hidx]
        m = jnp.maximum(jnp.max(s, axis=1, keepdims=True), sk)
        p = jnp.exp2(s - m)
        den = jnp.sum(p, axis=1, keepdims=True) + jnp.exp2(sk - m)
        o = jnp.dot(p.astype(BF16), v, preferred_element_type=F32) / den
        o_ref[:, g * HEAD_DIM:(g + 1) * HEAD_DIM] = o.astype(BF16)


def _win_attn(proj, slopes2, sink2, *, row0, batch, s_len, tq):
    nq = s_len // tq
    qb0 = row0 // tq
    kb0 = row0 // s_len
    gw = WIN_GROUP * HEAD_DIM
    q_col0 = (2 * DQ) // gw
    k_col0 = (2 * DQ + WQ) // HEAD_DIM
    v_col0 = (2 * DQ + WQ + WKV) // HEAD_DIM
    return pl.pallas_call(
        functools.partial(_win_attn_kernel, s_len=s_len, tq=tq),
        out_shape=jax.ShapeDtypeStruct((batch * s_len, WQ), BF16),
        grid=(batch, N_WIN_KV, nq),
        in_specs=[
            pl.BlockSpec(memory_space=pltpu.SMEM),
            pl.BlockSpec(memory_space=pltpu.SMEM),
            pl.BlockSpec((tq, gw), lambda b, kv, j: (qb0 + b * nq + j, q_col0 + kv)),
            pl.BlockSpec((s_len, HEAD_DIM), lambda b, kv, j: (kb0 + b, k_col0 + kv)),
            pl.BlockSpec((s_len, HEAD_DIM), lambda b, kv, j: (kb0 + b, v_col0 + kv)),
        ],
        out_specs=pl.BlockSpec((tq, gw), lambda b, kv, j: (b * nq + j, kv)),
        compiler_params=_cparams(3),
        name="win_attn",
    )(slopes2, sink2, proj, proj, proj)


def _pack_rows(v):
    half = v.shape[1] // 2
    lo = lax.bitcast_convert_type(v[:, :half].astype(BF16).astype(F32), jnp.uint32)
    hi = lax.bitcast_convert_type(v[:, half:].astype(BF16).astype(F32), jnp.uint32)
    return (lo >> 16) | hi


def _unpack_rows(u):
    lo = lax.bitcast_convert_type(u << 16, F32)
    hi = lax.bitcast_convert_type(u & jnp.uint32(0xFFFF0000), F32)
    return lo, hi


def _store_token_major(ref, u):
    rows, width = u.shape
    p = width // LANES
    for c in range(p):
        ref[pl.ds(c, rows, stride=p), :] = u[:, c * LANES:(c + 1) * LANES]


def _load_token_major(ref, rows):
    p = ref.shape[0] // rows
    return jnp.concatenate([ref[pl.ds(c, rows, stride=p), :] for c in range(p)], axis=1)


def _layer_norm(z, g, b):
    mu = jnp.mean(z, axis=1, keepdims=True)
    zc = z - mu
    var = jnp.mean(zc * zc, axis=1, keepdims=True)
    return zc * lax.rsqrt(var + LN_EPS) * g + b


def _outproj_kernel(hda_ref, hdb_ref, hwa_ref, hwb_ref, xa_ref, xb_ref, wod_ref, wow_ref,
                    g1_ref, lng_ref, lnb_ref, sc_ref, sh_ref, wrh_ref, wrl_ref, br_ref,
                    x1_ref, h2_ref, er_ref, w0_ref, w1_ref, cnt_ref, *, npb, alpha, tm):
    i = pl.program_id(0)
    x = _select_x(i, npb, xa_ref, xb_ref)
    hd = _select_x(i, npb, hda_ref, hdb_ref)
    hw = _select_x(i, npb, hwa_ref, hwb_ref)
    att = (jnp.dot(hd, wod_ref[...], preferred_element_type=F32)
           + jnp.dot(hw, wow_ref[...], preferred_element_type=F32))
    x1 = _layer_norm(alpha * x + g1_ref[...] * att, lng_ref[...], lnb_ref[...])
    x1_ref[...] = x1
    h2 = x1 * (1.0 + sc_ref[...]) + sh_ref[...]
    _store_token_major(h2_ref, _pack_rows(h2))

    h_hi = h2.astype(BF16)
    h_lo = (h2 - h_hi.astype(F32)).astype(BF16)
    lt = (jnp.dot(h_hi, wrh_ref[...], preferred_element_type=F32)
          + jnp.dot(h_lo, wrh_ref[...], preferred_element_type=F32)
          + jnp.dot(h_hi, wrl_ref[...], preferred_element_type=F32))
    lt = lt.T + br_ref[...]
    iota8 = lax.broadcasted_iota(jnp.int32, (SUBLANES, tm), 0)

    def first_argmax(v):
        vmax = jnp.max(v, axis=0, keepdims=True)
        idx = jnp.min(jnp.where(v == vmax, iota8, SUBLANES), axis=0, keepdims=True)
        return vmax, idx

    gl = lt[0:SUBLANES]
    gmax, g_idx = first_argmax(gl)
    g_w = 1.0 / jnp.sum(jnp.exp(gl - gmax), axis=0, keepdims=True)
    el = jnp.zeros((SUBLANES, tm), F32)
    for g in range(N_GROUPS):
        lo = SUBLANES + g * EXPERTS_PER_GROUP
        el = jnp.where(g_idx == g, lt[lo:lo + EXPERTS_PER_GROUP], el)
    v0, i0 = first_argmax(el)
    el2 = jnp.where(iota8 == i0, -jnp.inf, el)
    v1, i1 = first_argmax(el2)
    t = jnp.exp(v1 - v0)
    w0 = g_w / (1.0 + t)
    w1 = g_w * t / (1.0 + t)
    e0 = g_idx * EXPERTS_PER_GROUP + i0
    e1 = g_idx * EXPERTS_PER_GROUP + i1

    @pl.when(i == 0)
    def _():
        cnt_ref[...] = jnp.zeros_like(cnt_ref)

    iota_e = lax.broadcasted_iota(jnp.int32, (N_EXPERTS, tm), 0)
    hit0 = iota_e == e0
    hit1 = iota_e == e1
    c = jnp.where(hit0, 1.0, jnp.where(hit1, 1.0, 0.0))
    upper = (lax.broadcasted_iota(jnp.int32, (tm, tm), 0)
             < lax.broadcasted_iota(jnp.int32, (tm, tm), 1))
    before = jnp.dot(c.astype(BF16), jnp.where(upper, 1.0, 0.0).astype(BF16),
                     preferred_element_type=F32)
    tot = before + cnt_ref[:, 0:1]
    r0 = jnp.sum(jnp.where(hit0, tot, 0.0), axis=0, keepdims=True).astype(jnp.int32)
    r1 = jnp.sum(jnp.where(hit1, tot, 0.0), axis=0, keepdims=True).astype(jnp.int32)
    cnt_ref[...] = cnt_ref[...] + jnp.sum(c, axis=1, keepdims=True)

    er_ref[...] = jnp.where(iota8 == 0, e0, jnp.where(iota8 == 1, e1,
                            jnp.where(iota8 == 2, r0, jnp.where(iota8 == 3, r1, 0))))
    w0_ref[...] = jnp.broadcast_to(w0, (LANES, tm)).T
    w1_ref[...] = jnp.broadcast_to(w1, (LANES, tm)).T


def _outproj(lay, hd, hw, xa, xb, same_array, mod3, wo_d, wo_w, ln_g, ln_b, wr_hi, wr_lo, br, alpha, tm):
    d = xa.shape[1]
    p = d // (2 * LANES)
    const = dict(pipeline_mode=pl.Buffered(1))
    vec = lambda: pl.BlockSpec((1, d), lambda i: (0, 0))
    return pl.pallas_call(
        functools.partial(_outproj_kernel, npb=lay.tp // tm, alpha=alpha, tm=tm),
        out_shape=(jax.ShapeDtypeStruct((lay.t, d), F32),
                   jax.ShapeDtypeStruct((lay.t * p, LANES), jnp.uint32),
                   jax.ShapeDtypeStruct((SUBLANES, lay.t), jnp.int32),
                   jax.ShapeDtypeStruct((lay.t, LANES), F32),
                   jax.ShapeDtypeStruct((lay.t, LANES), F32),
                   jax.ShapeDtypeStruct((N_EXPERTS, LANES), F32)),
        grid=(lay.t // tm,),
        in_specs=_x_specs(lay, tm, DQ, False) + _x_specs(lay, tm, WQ, False)
        + _x_specs(lay, tm, d, same_array) + [
            pl.BlockSpec((DQ, d), lambda i: (0, 0), **const),
            pl.BlockSpec((WQ, d), lambda i: (0, 0), **const),
            _mod_spec(lay, tm, 2, d),
            vec(), vec(),
            _mod_spec(lay, tm, 4, d),
            _mod_spec(lay, tm, 3, d),
            pl.BlockSpec((d, LANES), lambda i: (0, 0)),
            pl.BlockSpec((d, LANES), lambda i: (0, 0)),
            pl.BlockSpec((LANES, 1), lambda i: (0, 0)),
        ],
        out_specs=(pl.BlockSpec((tm, d), lambda i: (i, 0)),
                   pl.BlockSpec((tm * p, LANES), lambda i: (i, 0)),
                   pl.BlockSpec((SUBLANES, tm), lambda i: (0, i)),
                   pl.BlockSpec((tm, LANES), lambda i: (i, 0)),
                   pl.BlockSpec((tm, LANES), lambda i: (i, 0)),
                   pl.BlockSpec((N_EXPERTS, LANES), lambda i: (0, 0))),
        compiler_params=_cparams(1),
        name="out_proj_ln_router",
    )(*hd, *hw, xa, xb, wo_d, wo_w, mod3, ln_g, ln_b, mod3, mod3, wr_hi, wr_lo, br)


def _token_copy(src, s_tok, dst, d_tok, sem, p):
    return pltpu.make_async_copy(src.at[pl.ds(pl.multiple_of(s_tok * p, p), p)],
                                 dst.at[pl.ds(pl.multiple_of(d_tok * p, p), p)], sem)


def _dispatch_kernel(zrow_ref, pos_ref, h_ref, xs_ref, zbuf, sem, zsem, *, ct, tm, p):
    i = pl.program_id(0)

    def zero_copy(row):
        start = pl.multiple_of(row * p, tm * p)
        return pltpu.make_async_copy(zbuf, xs_ref.at[pl.ds(start, tm * p)], zsem)

    @pl.when(i == 0)
    def _():
        zbuf[...] = jnp.zeros_like(zbuf)
        n_tiles = xs_ref.shape[0] // (tm * p)
        for e in range(N_EXPERTS):
            @pl.when(zrow_ref[e] >= 0)
            def _():
                zero_copy(zrow_ref[e]).start()
        lax.fori_loop(zrow_ref[N_EXPERTS], n_tiles, lambda k, c: (zero_copy(k * tm).start(), c)[1], 0)
        for e in range(N_EXPERTS):
            @pl.when(zrow_ref[e] >= 0)
            def _():
                zero_copy(zrow_ref[e]).wait()
        lax.fori_loop(zrow_ref[N_EXPERTS], n_tiles, lambda k, c: (zero_copy(k * tm).wait(), c)[1], 0)

    def body(j, carry):
        _token_copy(h_ref, j, xs_ref, pos_ref[2 * j], sem, p).start()
        _token_copy(h_ref, j, xs_ref, pos_ref[2 * j + 1], sem, p).start()
        return carry

    lax.fori_loop(0, ct, body, 0, unroll=8)
    for _ in range(2):
        pltpu.make_async_copy(h_ref, xs_ref.at[pl.ds(0, ct * p)], sem).wait()


def _dispatch(zrow, pos_flat, h2, n_rows, ct, tm, p):
    t = h2.shape[0] // p
    return pl.pallas_call(
        functools.partial(_dispatch_kernel, ct=ct, tm=tm, p=p),
        out_shape=jax.ShapeDtypeStruct((n_rows * p, LANES), jnp.uint32),
        grid=(t // ct,),
        in_specs=[
            pl.BlockSpec(memory_space=pltpu.SMEM),
            pl.BlockSpec((2 * ct,), lambda i: (i,), memory_space=pltpu.SMEM),
            pl.BlockSpec((ct * p, LANES), lambda i: (i, 0)),
        ],
        out_specs=pl.BlockSpec(memory_space=pl.ANY),
        scratch_shapes=[pltpu.VMEM((tm * p, LANES), jnp.uint32),
                        pltpu.SemaphoreType.DMA, pltpu.SemaphoreType.DMA],
        compiler_params=_cparams(1),
        name="moe_dispatch",
    )(zrow, pos_flat, h2)


def _experts_kernel(te_ref, nu_ref, x_ref, wg_ref, wu_ref, wd_ref, o_ref, *, tm):
    @pl.when(pl.program_id(0) < nu_ref[0])
    def _():
        lo, hi = _unpack_rows(_load_token_major(x_ref, tm))
        x = jnp.concatenate([lo.astype(BF16), hi.astype(BF16)], axis=1)
        g = jnp.dot(x, wg_ref[...], preferred_element_type=F32)
        u = jnp.dot(x, wu_ref[...], preferred_element_type=F32)
        a = (g * jax.nn.sigmoid(g) * u).astype(BF16)
        y = jnp.dot(a, wd_ref[...], preferred_element_type=F32)
        _store_token_major(o_ref, _pack_rows(y))

    @pl.when(pl.program_id(0) >= nu_ref[0])
    def _():
        o_ref[...] = jnp.zeros_like(o_ref)


def _experts(tile_expert, n_used, xs, wg, wu, wd, tm, p):
    d, f = wg.shape[1], wg.shape[2]
    n_rows = xs.shape[0] // p
    n_tiles = n_rows // tm

    def tile(i, te, nu):
        return jnp.minimum(i, nu[0] - 1)

    grid_spec = pltpu.PrefetchScalarGridSpec(
        num_scalar_prefetch=2,
        grid=(n_tiles,),
        in_specs=[
            pl.BlockSpec((tm * p, LANES), lambda i, te, nu: (tile(i, te, nu), 0)),
            pl.BlockSpec((None, d, f), lambda i, te, nu: (te[tile(i, te, nu)], 0, 0)),
            pl.BlockSpec((None, d, f), lambda i, te, nu: (te[tile(i, te, nu)], 0, 0)),
            pl.BlockSpec((None, f, d), lambda i, te, nu: (te[tile(i, te, nu)], 0, 0)),
        ],
        out_specs=pl.BlockSpec((tm * p, LANES), lambda i, te, nu: (i, 0)),
    )
    return pl.pallas_call(
        functools.partial(_experts_kernel, tm=tm),
        out_shape=jax.ShapeDtypeStruct((n_rows * p, LANES), jnp.uint32),
        grid_spec=grid_spec,
        compiler_params=_cparams(1),
        name="moe_experts",
    )(tile_expert, n_used, xs, wg, wu, wd)


def _final_kernel(posc_ref, posn_ref, x1_ref, ys_ref, w0_ref, w1_ref, g2_ref, lng_ref, lnb_ref, o_ref,
                  ya0, yb0, ya1, yb1, sems, *, alpha, tm, p, n_steps):
    i = pl.program_id(0)
    bufs = ((ya0, yb0), (ya1, yb1))

    def gather(pos_ref, s):
        def body(j, carry):
            _token_copy(ys_ref, pos_ref[2 * j], bufs[s][0], j, sems.at[0, s], p).start()
            _token_copy(ys_ref, pos_ref[2 * j + 1], bufs[s][1], j, sems.at[1, s], p).start()
            return carry
        lax.fori_loop(0, tm, body, 0, unroll=8)

    def wait(s):
        for k in range(2):
            pltpu.make_async_copy(ys_ref.at[pl.ds(0, tm * p)], bufs[s][k], sems.at[k, s]).wait()

    @pl.when(i == 0)
    def _():
        gather(posc_ref, 0)

    reps = x1_ref.shape[1] // LANES
    for s in range(2):
        @pl.when(i % 2 == s)
        def _():
            @pl.when(i + 1 < n_steps)
            def _():
                gather(posn_ref, 1 - s)
            wait(s)
            w0 = jnp.concatenate([w0_ref[...]] * reps, axis=1)
            w1 = jnp.concatenate([w1_ref[...]] * reps, axis=1)
            ya = jnp.concatenate(_unpack_rows(_load_token_major(bufs[s][0], tm)), axis=1)
            yb = jnp.concatenate(_unpack_rows(_load_token_major(bufs[s][1], tm)), axis=1)
            y = w0 * ya + w1 * yb
            o_ref[...] = _layer_norm(alpha * x1_ref[...] + g2_ref[...] * y, lng_ref[...], lnb_ref[...])


def _final(lay, pos_flat, x1, ys, w0, w1, mod3, ln_g, ln_b, alpha, tm, p, row0, n_rows):
    d = x1.shape[1]
    b0 = row0 // tm
    n_steps = n_rows // tm
    last = lay.t // tm - 1
    row = lambda i: (b0 + i, 0)
    vec = lambda: pl.BlockSpec((1, d), lambda i: (0, 0))
    slab = pltpu.VMEM((tm * p, LANES), jnp.uint32)
    return pl.pallas_call(
        functools.partial(_final_kernel, alpha=alpha, tm=tm, p=p, n_steps=n_steps),
        out_shape=jax.ShapeDtypeStruct((n_rows, d), F32),
        grid=(n_steps,),
        in_specs=[
            pl.BlockSpec((2 * tm,), lambda i: (b0 + i,), memory_space=pltpu.SMEM),
            pl.BlockSpec((2 * tm,), lambda i: (jnp.minimum(b0 + i + 1, last),), memory_space=pltpu.SMEM),
            pl.BlockSpec((tm, d), row),
            pl.BlockSpec(memory_space=pl.ANY),
            pl.BlockSpec((tm, LANES), row), pl.BlockSpec((tm, LANES), row),
            pl.BlockSpec((None, 1, d), lambda i: (lay.batch_of_tile(b0 + i, tm), 0, 5)),
            vec(), vec(),
        ],
        out_specs=pl.BlockSpec((tm, d), lambda i: (i, 0)),
        scratch_shapes=[slab, slab, slab, slab, pltpu.SemaphoreType.DMA((2, 2))],
        compiler_params=_cparams(1),
        name="moe_combine_ln",
    )(pos_flat, pos_flat, x1, ys, w0, w1, mod3, ln_g, ln_b)


def _routing_tables(er, cnt, tm, n_tiles):
    counts = cnt[:, 0].astype(jnp.int32)
    padded = ((counts + tm - 1) // tm) * tm
    ends = jnp.cumsum(padded)
    offs = ends - padded
    pos0 = offs[er[0]] + er[2]
    pos1 = offs[er[1]] + er[3]
    pos_flat = jnp.stack([pos0, pos1], axis=1).reshape(-1)
    tile_start = jnp.arange(n_tiles, dtype=jnp.int32) * tm
    tile_expert = jnp.minimum(
        jnp.sum((tile_start[:, None] >= ends[None, :]).astype(jnp.int32), axis=1), N_EXPERTS - 1)
    n_used = (ends[-1] // tm).reshape(1)
    zrow = jnp.concatenate([jnp.where(counts > 0, ends - tm, -1), n_used])
    return pos_flat, tile_expert.astype(jnp.int32), n_used.astype(jnp.int32), zrow.astype(jnp.int32)


def kernel(x_prompt, x_sample, c_prompt, c_sample, w_ada, b_ada, w_in, lam, subln_g, sink,
           w_o, ln_g, ln_b, w_rg, b_rg, w_re, b_re, w_gate, w_up, w_down):
    bp, sp, d = x_prompt.shape
    bs, ss, _ = x_sample.shape
    depth = w_ada.shape[0]
    lay = _Layout(bp, sp, bs, ss)
    alpha = (2.0 * depth) ** 0.25
    assert d % (2 * LANES * SUBLANES) == 0 and lay.tp % ss == 0
    p = d // (2 * LANES)

    tm = _pick(math.gcd(sp, ss), 512)
    tq_d = tm
    tk_d = tm
    tq_w = _pick(math.gcd(sp, ss), 512)
    while tq_w + 2 * WINDOW > min(sp, ss):
        tq_w //= 2
    ct = _pick(lay.t, 1024)
    n_tiles = (2 * lay.t + N_EXPERTS * (tm - 1) + tm - 1) // tm
    n_rows = n_tiles * tm

    xa = x_prompt.reshape(lay.tp, d)
    xb = x_sample.reshape(bs * ss, d)

    nb = bp + bs
    c_pad = jnp.zeros((-(-nb // SUBLANES) * SUBLANES, d), F32)
    c_pad = c_pad.at[:bp].set(c_prompt).at[bp:nb].set(c_sample)
    mod = _adaln(c_pad, w_ada, b_ada)

    slopes_d = jnp.asarray([LOG2E * 2.0 ** (-8.0 * (h + 1) / N_DIFF_HEADS) for h in range(N_DIFF_HEADS)], F32)
    slopes_w = jnp.asarray([LOG2E * 2.0 ** (-8.0 * (h + 1) / N_WIN_HEADS) for h in range(N_WIN_HEADS)], F32)
    tab_d, kaug_d = _alibi_tables(slopes_d, tk_d)
    cs = jnp.ones((2 * DQ + WQ + 2 * WKV,), F32)
    cs = cs.at[:DQ].set(LOG2E * DIFF_HALF ** -0.5).at[2 * DQ:2 * DQ + WQ].set(LOG2E * HEAD_DIM ** -0.5)
    cs = cs.reshape(1, -1)

    same = False
    x_last = None
    for l in range(depth):
        lam_init = 0.8 - 0.6 * math.exp(-0.3 * l)
        mod3 = mod[l].reshape(-1, 1, 6 * d)
        wl = w_in[l]
        w_main = jnp.concatenate([wl[:, :2 * DQ], wl[:, 3 * DQ:]], axis=1).astype(BF16)
        wvt = wl[:, 2 * DQ:3 * DQ].T.astype(BF16)
        proj, vt_all = _inproj(lay, xa, xb, same, mod3, w_main, wvt, cs, tm)

        g_col = subln_g[l].reshape(HEAD_DIM, 1)
        groups = (dict(row0=0, batch=bp, s_len=sp), dict(row0=lay.tp, batch=bs, s_len=ss))
        hd = [_diff_attn(proj, vt_all, slopes_d, tab_d, kaug_d, lam[l], g_col, lam_init=lam_init,
                         tq=tq_d, tk=tk_d, **g) for g in groups]
        sink2 = sink[l].astype(F32) * LOG2E
        hw = [_win_attn(proj, slopes_w, sink2, tq=tq_w, **g) for g in groups]

        wr = jnp.zeros((d, LANES), F32)
        wr = wr.at[:, :N_GROUPS].set(w_rg[l])
        wr = wr.at[:, SUBLANES:SUBLANES + N_EXPERTS].set(
            jnp.transpose(w_re[l], (1, 0, 2)).reshape(d, N_EXPERTS))
        wr_hi = wr.astype(BF16)
        wr_lo = (wr - wr_hi.astype(F32)).astype(BF16)
        br = jnp.full((LANES,), NEG_BIG, F32)
        br = br.at[:N_GROUPS].set(b_rg[l]).at[SUBLANES:SUBLANES + N_EXPERTS].set(b_re[l].reshape(-1))
        br = br.reshape(LANES, 1)

        wo = w_o[l].astype(BF16)
        x1, h2, er, w0, w1, cnt = _outproj(
            lay, hd, hw, xa, xb, same, mod3, wo[:DQ], wo[DQ:], ln_g[l, 0].reshape(1, d),
            ln_b[l, 0].reshape(1, d), wr_hi, wr_lo, br, alpha, tm)

        pos_flat, tile_expert, n_used, zrow = _routing_tables(er, cnt, tm, n_tiles)
        xs = _dispatch(zrow, pos_flat, h2, n_rows, ct, tm, p)
        ys = _experts(tile_expert, n_used, xs, w_gate[l].astype(BF16), w_up[l].astype(BF16),
                      w_down[l].astype(BF16), tm, p)
        fin = functools.partial(_final, lay, pos_flat, x1, ys, w0, w1, mod3, ln_g[l, 1].reshape(1, d),
                                ln_b[l, 1].reshape(1, d), alpha, tm, p)
        if l + 1 < depth:
            x_all = fin(0, lay.t)
            xa = xb = x_all
            same = True
        else:
            x_last = (fin(0, lay.tp), fin(lay.tp, lay.t - lay.tp))

    return (x_last[0].reshape(bp, sp, d), x_last[1].reshape(bs, ss, d))
```

```python
import functools
import math

import jax
import jax.numpy as jnp
from jax import lax
from jax.experimental import pallas as pl
from jax.experimental.pallas import tpu as pltpu

F32 = jnp.float32
BF16 = jnp.bfloat16

HEAD_DIM = 128
DIFF_HALF = HEAD_DIM // 2
N_DIFF_HEADS = 8
N_WIN_HEADS = 8
N_WIN_KV = 2
WIN_GROUP = N_WIN_HEADS // N_WIN_KV
WINDOW = 128
DQ = N_DIFF_HEADS * HEAD_DIM
WQ = N_WIN_HEADS * HEAD_DIM
WKV = N_WIN_KV * HEAD_DIM
N_GROUPS = 4
EXPERTS_PER_GROUP = 8
N_EXPERTS = N_GROUPS * EXPERTS_PER_GROUP
LN_EPS = 1e-5
RMS_EPS = 1e-5
LOG2E = 1.4426950408889634
NEG_BIG = -1e30

V7X_VMEM_BYTES = 64 * 1024 * 1024
VMEM_LIMIT = V7X_VMEM_BYTES - 8 * 1024 * 1024
LANES = 128
SUBLANES = 8
DIFF_TILE = 1024
OUTPROJ_CHUNKS = 2
ONES_ROWS = 16

_NT = (((1,), (1,)), ((), ()))


def _cparams(n_axes):
    return pltpu.CompilerParams(
        dimension_semantics=("arbitrary",) * n_axes, vmem_limit_bytes=VMEM_LIMIT)


def _pick(n, pref):
    t = min(pref, n)
    while n % t:
        t //= 2
    return t


def _adaln_kernel(c_ref, w_ref, b_ref, o_ref):
    c = c_ref[...]
    s = (c * jax.nn.sigmoid(c)).astype(BF16)
    o_ref[...] = jnp.dot(s, w_ref[...].astype(BF16), preferred_element_type=F32) + b_ref[...]


def _adaln(c_pad, w_ada, b_ada):
    depth, d, n = w_ada.shape
    tn = _pick(n, 1024)
    rows = c_pad.shape[0]
    return pl.pallas_call(
        _adaln_kernel,
        out_shape=jax.ShapeDtypeStruct((depth, rows, n), F32),
        grid=(depth, n // tn),
        in_specs=[
            pl.BlockSpec((rows, d), lambda l, j: (0, 0)),
            pl.BlockSpec((None, d, tn), lambda l, j: (l, 0, j)),
            pl.BlockSpec((None, 1, tn), lambda l, j: (l, 0, j)),
        ],
        out_specs=pl.BlockSpec((None, rows, tn), lambda l, j: (l, 0, j)),
        compiler_params=_cparams(2),
        name="adaln_mod",
    )(c_pad, w_ada, b_ada.reshape(depth, 1, n))


class _Layout:
    def __init__(self, bp, sp, bs, ss):
        self.bp, self.sp, self.bs, self.ss = bp, sp, bs, ss
        self.tp = bp * sp
        self.t = self.tp + bs * ss

    def batch_of_tile(self, i, tm):
        row = i * tm
        return jnp.where(row < self.tp, row // self.sp, self.bp + (row - self.tp) // self.ss)


def _mod_spec(lay, tm, chunk, d):
    return pl.BlockSpec((None, 1, d), lambda i: (lay.batch_of_tile(i, tm), 0, chunk))


def _x_specs(lay, tm, d, same_array):
    npb = lay.tp // tm
    boff = npb if same_array else 0
    return [
        pl.BlockSpec((tm, d), lambda i: (jnp.minimum(i, npb - 1), 0)),
        pl.BlockSpec((tm, d), lambda i: (jnp.maximum(i - npb, 0) + boff, 0)),
    ]


def _select_x(i, npb, xa_ref, xb_ref):
    return jnp.where(i < npb, xa_ref[...], xb_ref[...])


def _inproj_kernel(xa_ref, xb_ref, sc_ref, sh_ref, w_ref, wvt_ref, cs_ref, proj_ref, vt_ref, *, npb):
    x = _select_x(pl.program_id(0), npb, xa_ref, xb_ref)
    h = (x * (1.0 + sc_ref[...]) + sh_ref[...]).astype(BF16)
    acc = jnp.dot(h, w_ref[...], preferred_element_type=F32)
    proj_ref[...] = (acc * cs_ref[...]).astype(BF16)
    vt = lax.dot_general(wvt_ref[...], h, _NT, preferred_element_type=F32)
    vt_ref[...] = vt.astype(BF16)


def _inproj(lay, xa, xb, same_array, mod3, w_main, wvt, col_scale, tm):
    d = xa.shape[1]
    n = w_main.shape[1]
    nv = wvt.shape[0]
    const = dict(pipeline_mode=pl.Buffered(1))
    return pl.pallas_call(
        functools.partial(_inproj_kernel, npb=lay.tp // tm),
        out_shape=(jax.ShapeDtypeStruct((lay.t, n), BF16),
                   jax.ShapeDtypeStruct((lay.t // tm, nv, tm), BF16)),
        grid=(lay.t // tm,),
        in_specs=_x_specs(lay, tm, d, same_array) + [
            _mod_spec(lay, tm, 1, d),
            _mod_spec(lay, tm, 0, d),
            pl.BlockSpec((d, n), lambda i: (0, 0), **const),
            pl.BlockSpec((nv, d), lambda i: (0, 0), **const),
            pl.BlockSpec((1, n), lambda i: (0, 0)),
        ],
        out_specs=(pl.BlockSpec((tm, n), lambda i: (i, 0)),
                   pl.BlockSpec((None, nv, tm), lambda i: (i, 0, 0))),
        compiler_params=_cparams(1),
        name="in_proj",
    )(xa, xb, mod3, mod3, w_main, wvt, col_scale)


def _diff_attn_kernel(sl_ref, tab_ref, q_ref, k_ref, kaug_ref, vt_ref, lam_ref, g_ref, o_ref,
                      acc1, acc2, sa1, sa2, sb1, sb2, *, s_len, tq, tk, lam_init):
    nk = s_len // tk
    h = pl.program_id(1)
    j = pl.program_id(2)
    q = q_ref[...]
    lane = lax.broadcasted_iota(jnp.int32, q.shape, 1)
    zero = jnp.zeros_like(q)
    q1 = jnp.where(lane < DIFF_HALF, q, zero)
    q2 = jnp.where(lane >= DIFF_HALF, q, zero)
    slope = sl_ref[h]
    aug = jnp.broadcast_to(tab_ref[pl.ds(h, 1), :], q.shape).astype(BF16)
    qc1 = jnp.concatenate([q1, aug], axis=1)
    qc2 = jnp.concatenate([q2, aug], axis=1)
    kaug = kaug_ref[...]
    kaug_neg = -kaug
    rq = lax.broadcasted_iota(jnp.int32, (1, tq), 1).astype(F32)
    acc1[...] = jnp.zeros_like(acc1)
    acc2[...] = jnp.zeros_like(acc2)

    half = _pick(tq, 2 * LANES)

    def online(s_ref, mx, cq, m, acc, vt):
        m_new = jnp.maximum(m, mx + cq)
        a = jnp.exp2(m - m_new)
        shift = m_new - cq
        for c in range(tq // half):
            cols = slice(c * half, (c + 1) * half)
            p = jnp.exp2(s_ref[:, cols] - shift[:, cols]).astype(BF16)
            acc[:, cols] = a[:, cols] * acc[:, cols] + jnp.dot(vt, p, preferred_element_type=F32)
        return m_new

    def scores(t, buf1, buf2):
        kt = t + (t >= j).astype(jnp.int32)
        before = kt < j
        k = k_ref[pl.ds(pl.multiple_of(kt * tk, tk), tk), :]
        kc = jnp.concatenate([k, jnp.where(before, kaug, kaug_neg)], axis=1)
        s1 = lax.dot_general(kc, qc1, _NT, preferred_element_type=F32)
        s2 = lax.dot_general(kc, qc2, _NT, preferred_element_type=F32)
        buf1[...] = s1
        buf2[...] = s2
        cq = jnp.where(before, -slope, slope) * (rq + (j * tq - kt * tk).astype(F32))
        return kt, cq, jnp.max(s1, axis=0, keepdims=True), jnp.max(s2, axis=0, keepdims=True)

    def scores_diagonal(buf1, buf2):
        k = k_ref[pl.ds(pl.multiple_of(j * tk, tk), tk), :]
        dist = jnp.abs(lax.broadcasted_iota(jnp.int32, (tk, tq), 0)
                       - lax.broadcasted_iota(jnp.int32, (tk, tq), 1)).astype(F32)
        bias = slope * dist
        s1 = lax.dot_general(k, q1, _NT, preferred_element_type=F32) - bias
        s2 = lax.dot_general(k, q2, _NT, preferred_element_type=F32) - bias
        buf1[...] = s1
        buf2[...] = s2
        zer = jnp.zeros((1, tq), F32)
        return j, zer, jnp.max(s1, axis=0, keepdims=True), jnp.max(s2, axis=0, keepdims=True)

    ones = jnp.ones((ONES_ROWS, tk), BF16)

    def consume(tile, buf1, buf2, state):
        kt, cq, mx1, mx2 = tile
        m1, m2 = state
        n_slab = tk // vt_ref.shape[2]
        vt = jnp.concatenate([vt_ref[kt * n_slab + r] for r in range(n_slab)], axis=1)
        vt = jnp.concatenate([vt, ones], axis=0)
        return online(buf1, mx1, cq, m1, acc1, vt), online(buf2, mx2, cq, m2, acc2, vt)

    neg = jnp.full((1, tq), NEG_BIG, F32)
    state = (neg, neg)
    tile_a = scores_diagonal(sa1, sa2)
    n_pairs = (nk - 1) // 2

    def body(u, carry):
        state, tile_a = carry
        tile_b = scores(2 * u, sb1, sb2)
        state = consume(tile_a, sa1, sa2, state)
        tile_a = scores(2 * u + 1, sa1, sa2)
        state = consume(tile_b, sb1, sb2, state)
        return state, tile_a

    if n_pairs > 0:
        state, tile_a = lax.fori_loop(0, n_pairs, body, (state, tile_a))
    if nk % 2 == 0:
        tile_b = scores(nk - 2, sb1, sb2)
        state = consume(tile_a, sa1, sa2, state)
        consume(tile_b, sb1, sb2, state)
    else:
        consume(tile_a, sa1, sa2, state)

    lv = lam_ref[...]
    lam = (jnp.exp(jnp.sum(lv[0:1] * lv[1:2], axis=1, keepdims=True))
           - jnp.exp(jnp.sum(lv[2:3] * lv[3:4], axis=1, keepdims=True)) + lam_init)
    l1 = acc1[HEAD_DIM:HEAD_DIM + 1, :]
    l2 = acc2[HEAD_DIM:HEAD_DIM + 1, :]
    o = acc1[0:HEAD_DIM, :] / l1 - lam * (acc2[0:HEAD_DIM, :] / l2)
    ms = jnp.mean(o * o, axis=0, keepdims=True)
    o = o * lax.rsqrt(ms + RMS_EPS) * (1.0 - lam_init)
    o = o * g_ref[...]
    o_ref[...] = o.T.astype(BF16)


def _alibi_tables(slopes2, tk):
    assert tk <= 256 * 256
    r = jnp.arange(tk, dtype=jnp.int32)
    kaug = jnp.zeros((tk, LANES), F32)
    kaug = kaug.at[:, 0:3].set((r % 256).astype(F32)[:, None]).at[:, 3:6].set((r // 256).astype(F32)[:, None])
    s_a = slopes2.astype(BF16).astype(F32)
    s_b = (slopes2 - s_a).astype(BF16).astype(F32)
    s_c = (slopes2 - s_a - s_b).astype(BF16).astype(F32)
    pieces = jnp.stack([s_a, s_b, s_c], axis=1)
    tab = jnp.zeros((slopes2.shape[0], LANES), F32)
    tab = tab.at[:, 0:3].set(pieces).at[:, 3:6].set(256.0 * pieces)
    return tab, kaug.astype(BF16)


def _diff_attn(proj, vt_all, slopes2, tab, kaug, lam_l, g_col, *, row0, batch, s_len, tq, tk, lam_init):
    assert tq == tk
    slab = vt_all.shape[2]
    nq = s_len // tq
    qb0 = row0 // tq
    kb0 = row0 // s_len
    n_kh = DQ // HEAD_DIM
    return pl.pallas_call(
        functools.partial(_diff_attn_kernel, s_len=s_len, tq=tq, tk=tk, lam_init=lam_init),
        out_shape=jax.ShapeDtypeStruct((batch * s_len, DQ), BF16),
        grid=(batch, N_DIFF_HEADS, nq),
        in_specs=[
            pl.BlockSpec(memory_space=pltpu.SMEM),
            pl.BlockSpec(tab.shape, lambda b, h, j: (0, 0)),
            pl.BlockSpec((tq, HEAD_DIM), lambda b, h, j: (qb0 + b * nq + j, h)),
            pl.BlockSpec((s_len, HEAD_DIM), lambda b, h, j: (kb0 + b, n_kh + h)),
            pl.BlockSpec((tk, LANES), lambda b, h, j: (0, 0)),
            pl.BlockSpec((s_len // slab, HEAD_DIM, slab), lambda b, h, j: (kb0 + b, h, 0)),
            pl.BlockSpec((4, DIFF_HALF), lambda b, h, j: (0, 0)),
            pl.BlockSpec((HEAD_DIM, 1), lambda b, h, j: (0, 0)),
        ],
        out_specs=pl.BlockSpec((tq, HEAD_DIM), lambda b, h, j: (b * nq + j, h)),
        scratch_shapes=[pltpu.VMEM((HEAD_DIM + ONES_ROWS, tq), F32)] * 2 + [pltpu.VMEM((tk, tq), F32)] * 4,
        compiler_params=_cparams(3),
        name="diff_attn",
    )(slopes2, tab, proj, proj, kaug, vt_all, lam_l, g_col)


def _win_attn_kernel(sl_ref, sink_ref, q_ref, k_ref, v_ref, o_ref, *, s_len, tq):
    kv = pl.program_id(1)
    j = pl.program_id(2)
    w = tq + 2 * WINDOW
    ws = pl.multiple_of(jnp.clip(j * tq - WINDOW, 0, s_len - w), WINDOW)
    k = k_ref[pl.ds(ws, w), :]
    v = v_ref[pl.ds(ws, w), :]
    qpos = j * tq + lax.broadcasted_iota(jnp.int32, (tq, w), 0)
    kpos = ws + lax.broadcasted_iota(jnp.int32, (tq, w), 1)
    rel = jnp.abs(qpos - kpos)
    valid = rel <= WINDOW
    relf = rel.astype(F32)
    for g in range(WIN_GROUP):
        hidx = kv * WIN_GROUP + g
        qg = q_ref[:, g * HEAD_DIM:(g + 1) * HEAD_DIM]
        s = lax.dot_general(qg, k, _NT, preferred_element_type=F32)
        s = jnp.where(valid, s - sl_ref[hidx] * relf, NEG_BIG)
        sk = sink_ref[hidx]
        m = jnp.maximum(jnp.max(s, axis=1, keepdims=True), sk)
        p = jnp.exp2(s - m)
        den = jnp.sum(p, axis=1, keepdims=True) + jnp.exp2(sk - m)
        o = jnp.dot(p.astype(BF16), v, preferred_element_type=F32) / den
        o_ref[:, g * HEAD_DIM:(g + 1) * HEAD_DIM] = o.astype(BF16)


def _win_attn(proj, slopes2, sink2, *, row0, batch, s_len, tq):
    nq = s_len // tq
    qb0 = row0 // tq
    kb0 = row0 // s_len
    gw = WIN_GROUP * HEAD_DIM
    q_col0 = (2 * DQ) // gw
    k_col0 = (2 * DQ + WQ) // HEAD_DIM
    v_col0 = (2 * DQ + WQ + WKV) // HEAD_DIM
    return pl.pallas_call(
        functools.partial(_win_attn_kernel, s_len=s_len, tq=tq),
        out_shape=jax.ShapeDtypeStruct((batch * s_len, WQ), BF16),
        grid=(batch, N_WIN_KV, nq),
        in_specs=[
            pl.BlockSpec(memory_space=pltpu.SMEM),
            pl.BlockSpec(memory_space=pltpu.SMEM),
            pl.BlockSpec((tq, gw), lambda b, kv, j: (qb0 + b * nq + j, q_col0 + kv)),
            pl.BlockSpec((s_len, HEAD_DIM), lambda b, kv, j: (kb0 + b, k_col0 + kv)),
            pl.BlockSpec((s_len, HEAD_DIM), lambda b, kv, j: (kb0 + b, v_col0 + kv)),
        ],
        out_specs=pl.BlockSpec((tq, gw), lambda b, kv, j: (b * nq + j, kv)),
        compiler_params=_cparams(3),
        name="win_attn",
    )(slopes2, sink2, proj, proj, proj)


def _pack_rows(v):
    half = v.shape[1] // 2
    lo = lax.bitcast_convert_type(v[:, :half].astype(BF16).astype(F32), jnp.uint32)
    hi = lax.bitcast_convert_type(v[:, half:].astype(BF16).astype(F32), jnp.uint32)
    return (lo >> 16) | hi


def _unpack_rows(u):
    lo = lax.bitcast_convert_type(u << 16, F32)
    hi = lax.bitcast_convert_type(u & jnp.uint32(0xFFFF0000), F32)
    return lo, hi


def _store_token_major(ref, u):
    rows, width = u.shape
    p = width // LANES
    for c in range(p):
        ref[pl.ds(c, rows, stride=p), :] = u[:, c * LANES:(c + 1) * LANES]


def _load_token_major(ref, rows):
    p = ref.shape[0] // rows
    return jnp.concatenate([ref[pl.ds(c, rows, stride=p), :] for c in range(p)], axis=1)


def _layer_norm(z, g, b):
    mu = jnp.mean(z, axis=1, keepdims=True)
    zc = z - mu
    var = jnp.mean(zc * zc, axis=1, keepdims=True)
    return zc * lax.rsqrt(var + LN_EPS) * g + b


def _outproj_kernel(hda_ref, hdb_ref, hwa_ref, hwb_ref, xa_ref, xb_ref, wod_ref, wow_ref,
                    g1_ref, lng_ref, lnb_ref, sc_ref, sh_ref, wr2_ref, wrh_ref, br_ref,
                    x1_ref, h2_ref, er_ref, w0_ref, w1_ref, cnt_ref, *, npb, alpha, tm, n_chunks):
    i = pl.program_id(0)
    first = i < npb
    rows = tm // n_chunks
    p = h2_ref.shape[0] // tm

    @pl.when(i == 0)
    def _():
        cnt_ref[...] = jnp.zeros_like(cnt_ref)

    iota8 = lax.broadcasted_iota(jnp.int32, (SUBLANES, rows), 0)
    iota_e = lax.broadcasted_iota(jnp.int32, (N_EXPERTS, rows), 0)
    upper = jnp.where(lax.broadcasted_iota(jnp.int32, (rows, rows), 0)
                      < lax.broadcasted_iota(jnp.int32, (rows, rows), 1), 1.0, 0.0).astype(BF16)

    def first_argmax(v):
        vmax = jnp.max(v, axis=0, keepdims=True)
        idx = jnp.min(jnp.where(v == vmax, iota8, SUBLANES), axis=0, keepdims=True)
        return vmax, idx

    for c in range(n_chunks):
        rs = slice(c * rows, (c + 1) * rows)
        x = jnp.where(first, xa_ref[rs, :], xb_ref[rs, :])
        hd = jnp.where(first, hda_ref[rs, :], hdb_ref[rs, :])
        hw = jnp.where(first, hwa_ref[rs, :], hwb_ref[rs, :])
        att = (jnp.dot(hd, wod_ref[...], preferred_element_type=F32)
               + jnp.dot(hw, wow_ref[...], preferred_element_type=F32))
        x1 = _layer_norm(alpha * x + g1_ref[...] * att, lng_ref[...], lnb_ref[...])
        x1_ref[rs, :] = x1
        h2 = x1 * (1.0 + sc_ref[...]) + sh_ref[...]
        _store_token_major(h2_ref.at[pl.ds(c * rows * p, rows * p)], _pack_rows(h2))

        h_hi = h2.astype(BF16)
        h_lo = (h2 - h_hi.astype(F32)).astype(BF16)
        both = jnp.dot(h_hi, wr2_ref[...], preferred_element_type=F32)
        lt = (both[:, :LANES] + both[:, LANES:]
              + jnp.dot(h_lo, wrh_ref[...], preferred_element_type=F32))
        lt = lt.T + br_ref[...]

        gl = lt[0:SUBLANES]
        gmax, g_idx = first_argmax(gl)
        g_w = 1.0 / jnp.sum(jnp.exp(gl - gmax), axis=0, keepdims=True)
        el = jnp.zeros((SUBLANES, rows), F32)
        for g in range(N_GROUPS):
            lo = SUBLANES + g * EXPERTS_PER_GROUP
            el = jnp.where(g_idx == g, lt[lo:lo + EXPERTS_PER_GROUP], el)
        v0, i0 = first_argmax(el)
        el2 = jnp.where(iota8 == i0, -jnp.inf, el)
        v1, i1 = first_argmax(el2)
        t = jnp.exp(v1 - v0)
        w0 = g_w / (1.0 + t)
        w1 = g_w * t / (1.0 + t)
        e0 = g_idx * EXPERTS_PER_GROUP + i0
        e1 = g_idx * EXPERTS_PER_GROUP + i1

        hit0 = iota_e == e0
        hit1 = iota_e == e1
        cnt = jnp.where(hit0, 1.0, jnp.where(hit1, 1.0, 0.0))
        before = jnp.dot(cnt.astype(BF16), upper, preferred_element_type=F32)
        tot = before + cnt_ref[:, 0:1]
        r0 = jnp.sum(jnp.where(hit0, tot, 0.0), axis=0, keepdims=True).astype(jnp.int32)
        r1 = jnp.sum(jnp.where(hit1, tot, 0.0), axis=0, keepdims=True).astype(jnp.int32)
        cnt_ref[...] = cnt_ref[...] + jnp.sum(cnt, axis=1, keepdims=True)

        er_ref[:, rs] = jnp.where(iota8 == 0, e0, jnp.where(iota8 == 1, e1,
                                  jnp.where(iota8 == 2, r0, jnp.where(iota8 == 3, r1, 0))))
        w0_ref[rs, :] = jnp.broadcast_to(w0, (LANES, rows)).T
        w1_ref[rs, :] = jnp.broadcast_to(w1, (LANES, rows)).T


def _outproj(lay, hd, hw, xa, xb, same_array, mod3, wo_d, wo_w, ln_g, ln_b, wr_hi, wr_lo, br, alpha, tm):
    d = xa.shape[1]
    p = d // (2 * LANES)
    const = dict(pipeline_mode=pl.Buffered(1))
    vec = lambda: pl.BlockSpec((1, d), lambda i: (0, 0))
    return pl.pallas_call(
        functools.partial(_outproj_kernel, npb=lay.tp // tm, alpha=alpha, tm=tm,
                          n_chunks=OUTPROJ_CHUNKS),
        out_shape=(jax.ShapeDtypeStruct((lay.t, d), F32),
                   jax.ShapeDtypeStruct((lay.t * p, LANES), jnp.uint32),
                   jax.ShapeDtypeStruct((SUBLANES, lay.t), jnp.int32),
                   jax.ShapeDtypeStruct((lay.t, LANES), F32),
                   jax.ShapeDtypeStruct((lay.t, LANES), F32),
                   jax.ShapeDtypeStruct((N_EXPERTS, LANES), F32)),
        grid=(lay.t // tm,),
        in_specs=_x_specs(lay, tm, DQ, False) + _x_specs(lay, tm, WQ, False)
        + _x_specs(lay, tm, d, same_array) + [
            pl.BlockSpec((DQ, d), lambda i: (0, 0), **const),
            pl.BlockSpec((WQ, d), lambda i: (0, 0), **const),
            _mod_spec(lay, tm, 2, d),
            vec(), vec(),
            _mod_spec(lay, tm, 4, d),
            _mod_spec(lay, tm, 3, d),
            pl.BlockSpec((d, 2 * LANES), lambda i: (0, 0)),
            pl.BlockSpec((d, LANES), lambda i: (0, 0)),
            pl.BlockSpec((LANES, 1), lambda i: (0, 0)),
        ],
        out_specs=(pl.BlockSpec((tm, d), lambda i: (i, 0)),
                   pl.BlockSpec((tm * p, LANES), lambda i: (i, 0)),
                   pl.BlockSpec((SUBLANES, tm), lambda i: (0, i)),
                   pl.BlockSpec((tm, LANES), lambda i: (i, 0)),
                   pl.BlockSpec((tm, LANES), lambda i: (i, 0)),
                   pl.BlockSpec((N_EXPERTS, LANES), lambda i: (0, 0))),
        compiler_params=_cparams(1),
        name="out_proj_ln_router",
    )(*hd, *hw, xa, xb, wo_d, wo_w, mod3, ln_g, ln_b, mod3, mod3,
      jnp.concatenate([wr_hi, wr_lo], axis=1), wr_hi, br)


def _token_copy(src, s_tok, dst, d_tok, sem, p):
    return pltpu.make_async_copy(src.at[pl.ds(pl.multiple_of(s_tok * p, p), p)],
                                 dst.at[pl.ds(pl.multiple_of(d_tok * p, p), p)], sem)


def _dispatch_kernel(zrow_ref, pos_ref, h_ref, xs_ref, zbuf, sem, zsem, *, ct, tm, p):
    i = pl.program_id(0)

    def zero_copy(row):
        start = pl.multiple_of(row * p, tm * p)
        return pltpu.make_async_copy(zbuf, xs_ref.at[pl.ds(start, tm * p)], zsem)

    @pl.when(i == 0)
    def _():
        zbuf[...] = jnp.zeros_like(zbuf)
        n_tiles = xs_ref.shape[0] // (tm * p)
        for e in range(N_EXPERTS):
            @pl.when(zrow_ref[e] >= 0)
            def _():
                zero_copy(zrow_ref[e]).start()
        lax.fori_loop(zrow_ref[N_EXPERTS], n_tiles, lambda k, c: (zero_copy(k * tm).start(), c)[1], 0)
        for e in range(N_EXPERTS):
            @pl.when(zrow_ref[e] >= 0)
            def _():
                zero_copy(zrow_ref[e]).wait()
        lax.fori_loop(zrow_ref[N_EXPERTS], n_tiles, lambda k, c: (zero_copy(k * tm).wait(), c)[1], 0)

    def body(j, carry):
        _token_copy(h_ref, j, xs_ref, pos_ref[2 * j], sem, p).start()
        _token_copy(h_ref, j, xs_ref, pos_ref[2 * j + 1], sem, p).start()
        return carry

    lax.fori_loop(0, ct, body, 0, unroll=8)
    for _ in range(2):
        pltpu.make_async_copy(h_ref, xs_ref.at[pl.ds(0, ct * p)], sem).wait()


def _dispatch(zrow, pos_flat, h2, n_rows, ct, tm, p):
    t = h2.shape[0] // p
    return pl.pallas_call(
        functools.partial(_dispatch_kernel, ct=ct, tm=tm, p=p),
        out_shape=jax.ShapeDtypeStruct((n_rows * p, LANES), jnp.uint32),
        grid=(t // ct,),
        in_specs=[
            pl.BlockSpec(memory_space=pltpu.SMEM),
            pl.BlockSpec((2 * ct,), lambda i: (i,), memory_space=pltpu.SMEM),
            pl.BlockSpec((ct * p, LANES), lambda i: (i, 0)),
        ],
        out_specs=pl.BlockSpec(memory_space=pl.ANY),
        scratch_shapes=[pltpu.VMEM((tm * p, LANES), jnp.uint32),
                        pltpu.SemaphoreType.DMA, pltpu.SemaphoreType.DMA],
        compiler_params=_cparams(1),
        name="moe_dispatch",
    )(zrow, pos_flat, h2)


def _experts_kernel(te_ref, nu_ref, x_ref, wg_ref, wu_ref, wd_ref, o_ref, wg_b, wu_b, wd_b, *, tm):
    i = pl.program_id(0)

    @pl.when(i < nu_ref[0])
    def _():
        @pl.when((i == 0) | (te_ref[i] != te_ref[jnp.maximum(i - 1, 0)]))
        def _():
            wg_b[...] = wg_ref[...].astype(BF16)
            wu_b[...] = wu_ref[...].astype(BF16)
            wd_b[...] = wd_ref[...].astype(BF16)

        lo, hi = _unpack_rows(_load_token_major(x_ref, tm))
        x = jnp.concatenate([lo.astype(BF16), hi.astype(BF16)], axis=1)
        g = jnp.dot(x, wg_b[...], preferred_element_type=F32)
        u = jnp.dot(x, wu_b[...], preferred_element_type=F32)
        a = (g * jax.nn.sigmoid(g) * u).astype(BF16)
        y = jnp.dot(a, wd_b[...], preferred_element_type=F32)
        _store_token_major(o_ref, _pack_rows(y))

    @pl.when(pl.program_id(0) >= nu_ref[0])
    def _():
        o_ref[...] = jnp.zeros_like(o_ref)


def _experts(tile_expert, n_used, xs, wg, wu, wd, layer, tm, p):
    d, f = wg.shape[2], wg.shape[3]
    n_rows = xs.shape[0] // p
    n_tiles = n_rows // tm

    def tile(i, te, nu):
        return jnp.minimum(i, nu[0] - 1)

    grid_spec = pltpu.PrefetchScalarGridSpec(
        num_scalar_prefetch=2,
        grid=(n_tiles,),
        in_specs=[
            pl.BlockSpec((tm * p, LANES), lambda i, te, nu: (tile(i, te, nu), 0)),
            pl.BlockSpec((None, None, d, f), lambda i, te, nu: (layer, te[tile(i, te, nu)], 0, 0)),
            pl.BlockSpec((None, None, d, f), lambda i, te, nu: (layer, te[tile(i, te, nu)], 0, 0)),
            pl.BlockSpec((None, None, f, d), lambda i, te, nu: (layer, te[tile(i, te, nu)], 0, 0)),
        ],
        out_specs=pl.BlockSpec((tm * p, LANES), lambda i, te, nu: (i, 0)),
        scratch_shapes=[pltpu.VMEM((d, f), BF16), pltpu.VMEM((d, f), BF16), pltpu.VMEM((f, d), BF16)],
    )
    return pl.pallas_call(
        functools.partial(_experts_kernel, tm=tm),
        out_shape=jax.ShapeDtypeStruct((n_rows * p, LANES), jnp.uint32),
        grid_spec=grid_spec,
        compiler_params=_cparams(1),
        name="moe_experts",
    )(tile_expert, n_used, xs, wg, wu, wd)


def _final_kernel(posc_ref, posn_ref, x1_ref, ys_ref, w0_ref, w1_ref, g2_ref, lng_ref, lnb_ref, o_ref,
                  ya0, yb0, ya1, yb1, sems, *, alpha, tm, p, n_steps):
    i = pl.program_id(0)
    bufs = ((ya0, yb0), (ya1, yb1))

    def gather(pos_ref, s):
        def body(j, carry):
            _token_copy(ys_ref, pos_ref[2 * j], bufs[s][0], j, sems.at[0, s], p).start()
            _token_copy(ys_ref, pos_ref[2 * j + 1], bufs[s][1], j, sems.at[1, s], p).start()
            return carry
        lax.fori_loop(0, tm, body, 0, unroll=8)

    def wait(s):
        for k in range(2):
            pltpu.make_async_copy(ys_ref.at[pl.ds(0, tm * p)], bufs[s][k], sems.at[k, s]).wait()

    @pl.when(i == 0)
    def _():
        gather(posc_ref, 0)

    reps = x1_ref.shape[1] // LANES
    for s in range(2):
        @pl.when(i % 2 == s)
        def _():
            @pl.when(i + 1 < n_steps)
            def _():
                gather(posn_ref, 1 - s)
            wait(s)
            w0 = jnp.concatenate([w0_ref[...]] * reps, axis=1)
            w1 = jnp.concatenate([w1_ref[...]] * reps, axis=1)
            ya = jnp.concatenate(_unpack_rows(_load_token_major(bufs[s][0], tm)), axis=1)
            yb = jnp.concatenate(_unpack_rows(_load_token_major(bufs[s][1], tm)), axis=1)
            y = w0 * ya + w1 * yb
            o_ref[...] = _layer_norm(alpha * x1_ref[...] + g2_ref[...] * y, lng_ref[...], lnb_ref[...])


def _final(lay, pos_flat, x1, ys, w0, w1, mod3, ln_g, ln_b, alpha, tm, p, row0, n_rows):
    d = x1.shape[1]
    b0 = row0 // tm
    n_steps = n_rows // tm
    last = lay.t // tm - 1
    row = lambda i: (b0 + i, 0)
    vec = lambda: pl.BlockSpec((1, d), lambda i: (0, 0))
    slab = pltpu.VMEM((tm * p, LANES), jnp.uint32)
    return pl.pallas_call(
        functools.partial(_final_kernel, alpha=alpha, tm=tm, p=p, n_steps=n_steps),
        out_shape=jax.ShapeDtypeStruct((n_rows, d), F32),
        grid=(n_steps,),
        in_specs=[
            pl.BlockSpec((2 * tm,), lambda i: (b0 + i,), memory_space=pltpu.SMEM),
            pl.BlockSpec((2 * tm,), lambda i: (jnp.minimum(b0 + i + 1, last),), memory_space=pltpu.SMEM),
            pl.BlockSpec((tm, d), row),
            pl.BlockSpec(memory_space=pl.ANY),
            pl.BlockSpec((tm, LANES), row), pl.BlockSpec((tm, LANES), row),
            pl.BlockSpec((None, 1, d), lambda i: (lay.batch_of_tile(b0 + i, tm), 0, 5)),
            vec(), vec(),
        ],
        out_specs=pl.BlockSpec((tm, d), lambda i: (i, 0)),
        scratch_shapes=[slab, slab, slab, slab, pltpu.SemaphoreType.DMA((2, 2))],
        compiler_params=_cparams(1),
        name="moe_combine_ln",
    )(pos_flat, pos_flat, x1, ys, w0, w1, mod3, ln_g, ln_b)


def _routing_tables(er, cnt, tm, n_tiles):
    counts = cnt[:, 0].astype(jnp.int32)
    padded = ((counts + tm - 1) // tm) * tm
    ends = jnp.cumsum(padded)
    offs = ends - padded
    pos0 = offs[er[0]] + er[2]
    pos1 = offs[er[1]] + er[3]
    pos_flat = jnp.stack([pos0, pos1], axis=1).reshape(-1)
    tile_start = jnp.arange(n_tiles, dtype=jnp.int32) * tm
    tile_expert = jnp.minimum(
        jnp.sum((tile_start[:, None] >= ends[None, :]).astype(jnp.int32), axis=1), N_EXPERTS - 1)
    n_used = (ends[-1] // tm).reshape(1)
    zrow = jnp.concatenate([jnp.where(counts > 0, ends - tm, -1), n_used])
    return pos_flat, tile_expert.astype(jnp.int32), n_used.astype(jnp.int32), zrow.astype(jnp.int32)


def kernel(x_prompt, x_sample, c_prompt, c_sample, w_ada, b_ada, w_in, lam, subln_g, sink,
           w_o, ln_g, ln_b, w_rg, b_rg, w_re, b_re, w_gate, w_up, w_down):
    bp, sp, d = x_prompt.shape
    bs, ss, _ = x_sample.shape
    depth = w_ada.shape[0]
    lay = _Layout(bp, sp, bs, ss)
    alpha = (2.0 * depth) ** 0.25
    assert d % (2 * LANES * SUBLANES) == 0 and lay.tp % ss == 0
    p = d // (2 * LANES)

    tm = _pick(math.gcd(sp, ss), 512)
    tq_d = tk_d = _pick(math.gcd(sp, ss), DIFF_TILE)
    tq_w = _pick(math.gcd(sp, ss), 2 * WINDOW)
    while tq_w + 2 * WINDOW > min(sp, ss):
        tq_w //= 2
    ct = _pick(lay.t, 1024)
    n_tiles = (2 * lay.t + N_EXPERTS * (tm - 1) + tm - 1) // tm
    n_rows = n_tiles * tm

    xa = x_prompt.reshape(lay.tp, d)
    xb = x_sample.reshape(bs * ss, d)

    nb = bp + bs
    c_pad = jnp.zeros((-(-nb // SUBLANES) * SUBLANES, d), F32)
    c_pad = c_pad.at[:bp].set(c_prompt).at[bp:nb].set(c_sample)
    mod = _adaln(c_pad, w_ada, b_ada)

    slopes_d = jnp.asarray([LOG2E * 2.0 ** (-8.0 * (h + 1) / N_DIFF_HEADS) for h in range(N_DIFF_HEADS)], F32)
    slopes_w = jnp.asarray([LOG2E * 2.0 ** (-8.0 * (h + 1) / N_WIN_HEADS) for h in range(N_WIN_HEADS)], F32)
    tab_d, kaug_d = _alibi_tables(slopes_d, tk_d)
    cs = jnp.ones((2 * DQ + WQ + 2 * WKV,), F32)
    cs = cs.at[:DQ].set(LOG2E * DIFF_HALF ** -0.5).at[2 * DQ:2 * DQ + WQ].set(LOG2E * HEAD_DIM ** -0.5)
    cs = cs.reshape(1, -1)

    same = False
    x_last = None
    for l in range(depth):
        lam_init = 0.8 - 0.6 * math.exp(-0.3 * l)
        mod3 = mod[l].reshape(-1, 1, 6 * d)
        wl = w_in[l]
        w_main = jnp.concatenate([wl[:, :2 * DQ], wl[:, 3 * DQ:]], axis=1).astype(BF16)
        wvt = wl[:, 2 * DQ:3 * DQ].T.astype(BF16)
        proj, vt_all = _inproj(lay, xa, xb, same, mod3, w_main, wvt, cs, tm)

        g_col = subln_g[l].reshape(HEAD_DIM, 1)
        groups = (dict(row0=0, batch=bp, s_len=sp), dict(row0=lay.tp, batch=bs, s_len=ss))
        hd = [_diff_attn(proj, vt_all, slopes_d, tab_d, kaug_d, lam[l], g_col, lam_init=lam_init,
                         tq=tq_d, tk=tk_d, **g) for g in groups]
        sink2 = sink[l].astype(F32) * LOG2E
        hw = [_win_attn(proj, slopes_w, sink2, tq=tq_w, **g) for g in groups]

        wr = jnp.zeros((d, LANES), F32)
        wr = wr.at[:, :N_GROUPS].set(w_rg[l])
        wr = wr.at[:, SUBLANES:SUBLANES + N_EXPERTS].set(
            jnp.transpose(w_re[l], (1, 0, 2)).reshape(d, N_EXPERTS))
        wr_hi = wr.astype(BF16)
        wr_lo = (wr - wr_hi.astype(F32)).astype(BF16)
        br = jnp.full((LANES,), NEG_BIG, F32)
        br = br.at[:N_GROUPS].set(b_rg[l]).at[SUBLANES:SUBLANES + N_EXPERTS].set(b_re[l].reshape(-1))
        br = br.reshape(LANES, 1)

        wo = w_o[l].astype(BF16)
        x1, h2, er, w0, w1, cnt = _outproj(
            lay, hd, hw, xa, xb, same, mod3, wo[:DQ], wo[DQ:], ln_g[l, 0].reshape(1, d),
            ln_b[l, 0].reshape(1, d), wr_hi, wr_lo, br, alpha, tm)

        pos_flat, tile_expert, n_used, zrow = _routing_tables(er, cnt, tm, n_tiles)
        xs = _dispatch(zrow, pos_flat, h2, n_rows, ct, tm, p)
        ys = _experts(tile_expert, n_used, xs, w_gate, w_up, w_down, l, tm, p)
        fin = functools.partial(_final, lay, pos_flat, x1, ys, w0, w1, mod3, ln_g[l, 1].reshape(1, d),
                                ln_b[l, 1].reshape(1, d), alpha, tm, p)
        if l + 1 < depth:
            x_all = fin(0, lay.t)
            xa = xb = x_all
            same = True
        else:
            x_last = (fin(0, lay.tp), fin(lay.tp, lay.t - lay.tp))

    return (x_last[0].reshape(bp, sp, d), x_last[1].reshape(bs, ss, d))
```

```python
import functools
import math

import jax
import jax.numpy as jnp
from jax import lax
from jax.experimental import pallas as pl
from jax.experimental.pallas import tpu as pltpu

F32 = jnp.float32
BF16 = jnp.bfloat16

HEAD_DIM = 128
DIFF_HALF = HEAD_DIM // 2
N_DIFF_HEADS = 8
N_WIN_HEADS = 8
N_WIN_KV = 2
WIN_GROUP = N_WIN_HEADS // N_WIN_KV
WINDOW = 128
DQ = N_DIFF_HEADS * HEAD_DIM
WQ = N_WIN_HEADS * HEAD_DIM
WKV = N_WIN_KV * HEAD_DIM
N_GROUPS = 4
EXPERTS_PER_GROUP = 8
N_EXPERTS = N_GROUPS * EXPERTS_PER_GROUP
LN_EPS = 1e-5
RMS_EPS = 1e-5
LOG2E = 1.4426950408889634
NEG_BIG = -1e30
FAR = 1e30

V7X_VMEM_BYTES = 64 * 1024 * 1024
VMEM_LIMIT = V7X_VMEM_BYTES - 8 * 1024 * 1024
LANES = 128
SUBLANES = 8
DIFF_TILE = 1024
OUTPROJ_CHUNKS = 2
ONES_ROWS = 16

_NT = (((1,), (1,)), ((), ()))


def _cparams(n_axes):
    return pltpu.CompilerParams(
        dimension_semantics=("arbitrary",) * n_axes, vmem_limit_bytes=VMEM_LIMIT)


def _pick(n, pref):
    t = min(pref, n)
    while n % t:
        t //= 2
    return t


def _adaln_kernel(c_ref, w_ref, b_ref, o_ref):
    c = c_ref[...]
    s = (c * jax.nn.sigmoid(c)).astype(BF16)
    o_ref[...] = jnp.dot(s, w_ref[...].astype(BF16), preferred_element_type=F32) + b_ref[...]


def _adaln(c_pad, w_ada, b_ada):
    depth, d, n = w_ada.shape
    tn = _pick(n, 1024)
    rows = c_pad.shape[0]
    return pl.pallas_call(
        _adaln_kernel,
        out_shape=jax.ShapeDtypeStruct((depth, rows, n), F32),
        grid=(depth, n // tn),
        in_specs=[
            pl.BlockSpec((rows, d), lambda l, j: (0, 0)),
            pl.BlockSpec((None, d, tn), lambda l, j: (l, 0, j)),
            pl.BlockSpec((None, 1, tn), lambda l, j: (l, 0, j)),
        ],
        out_specs=pl.BlockSpec((None, rows, tn), lambda l, j: (l, 0, j)),
        compiler_params=_cparams(2),
        name="adaln_mod",
    )(c_pad, w_ada, b_ada.reshape(depth, 1, n))


class _Layout:
    def __init__(self, bp, sp, bs, ss):
        self.bp, self.sp, self.bs, self.ss = bp, sp, bs, ss
        self.tp = bp * sp
        self.t = self.tp + bs * ss

    def batch_of_tile(self, i, tm):
        row = i * tm
        return jnp.where(row < self.tp, row // self.sp, self.bp + (row - self.tp) // self.ss)


def _mod_spec(lay, tm, chunk, d):
    return pl.BlockSpec((None, 1, d), lambda i: (lay.batch_of_tile(i, tm), 0, chunk))


def _x_specs(lay, tm, d, same_array):
    npb = lay.tp // tm
    boff = npb if same_array else 0
    return [
        pl.BlockSpec((tm, d), lambda i: (jnp.minimum(i, npb - 1), 0)),
        pl.BlockSpec((tm, d), lambda i: (jnp.maximum(i - npb, 0) + boff, 0)),
    ]


def _select_x(i, npb, xa_ref, xb_ref):
    return jnp.where(i < npb, xa_ref[...], xb_ref[...])


def _inproj_kernel(xa_ref, xb_ref, sc_ref, sh_ref, w_ref, wvt_ref, cs_ref, proj_ref, vt_ref, *, npb):
    x = _select_x(pl.program_id(0), npb, xa_ref, xb_ref)
    h = (x * (1.0 + sc_ref[...]) + sh_ref[...]).astype(BF16)
    acc = jnp.dot(h, w_ref[...], preferred_element_type=F32)
    proj_ref[...] = (acc * cs_ref[...]).astype(BF16)
    vt = lax.dot_general(wvt_ref[...], h, _NT, preferred_element_type=F32)
    vt_ref[...] = vt.astype(BF16)


def _inproj(lay, xa, xb, same_array, mod3, w_main, wvt, col_scale, tm):
    d = xa.shape[1]
    n = w_main.shape[1]
    nv = wvt.shape[0]
    const = dict(pipeline_mode=pl.Buffered(1))
    return pl.pallas_call(
        functools.partial(_inproj_kernel, npb=lay.tp // tm),
        out_shape=(jax.ShapeDtypeStruct((lay.t, n), BF16),
                   jax.ShapeDtypeStruct((lay.t // tm, nv, tm), BF16)),
        grid=(lay.t // tm,),
        in_specs=_x_specs(lay, tm, d, same_array) + [
            _mod_spec(lay, tm, 1, d),
            _mod_spec(lay, tm, 0, d),
            pl.BlockSpec((d, n), lambda i: (0, 0), **const),
            pl.BlockSpec((nv, d), lambda i: (0, 0), **const),
            pl.BlockSpec((1, n), lambda i: (0, 0)),
        ],
        out_specs=(pl.BlockSpec((tm, n), lambda i: (i, 0)),
                   pl.BlockSpec((None, nv, tm), lambda i: (i, 0, 0))),
        compiler_params=_cparams(1),
        name="in_proj",
    )(xa, xb, mod3, mod3, w_main, wvt, col_scale)


def _diff_attn_kernel(sl_ref, tab_ref, q_ref, k_ref, kaug_ref, vt_ref, lam_ref, g_ref, o_ref,
                      acc1, acc2, sa1, sa2, sb1, sb2, *, s_len, tq, tk, lam_init):
    nk = s_len // tk
    h = pl.program_id(1)
    j = pl.program_id(2)
    q = q_ref[...]
    lane = lax.broadcasted_iota(jnp.int32, q.shape, 1)
    zero = jnp.zeros_like(q)
    q1 = jnp.where(lane < DIFF_HALF, q, zero)
    q2 = jnp.where(lane >= DIFF_HALF, q, zero)
    slope = sl_ref[h]
    aug = jnp.broadcast_to(tab_ref[pl.ds(h, 1), :], q.shape).astype(BF16)
    qc1 = jnp.concatenate([q1, aug], axis=1)
    qc2 = jnp.concatenate([q2, aug], axis=1)
    kaug = kaug_ref[...]
    kaug_neg = -kaug
    rq = lax.broadcasted_iota(jnp.int32, (1, tq), 1).astype(F32)
    acc1[...] = jnp.zeros_like(acc1)
    acc2[...] = jnp.zeros_like(acc2)

    half = _pick(tq, 2 * LANES)

    def online(s_ref, mx, cq, m, acc, vt):
        m_new = jnp.maximum(m, mx + cq)
        a = jnp.exp2(m - m_new)
        shift = m_new - cq
        for c in range(tq // half):
            cols = slice(c * half, (c + 1) * half)
            p = jnp.exp2(s_ref[:, cols] - shift[:, cols]).astype(BF16)
            acc[:, cols] = a[:, cols] * acc[:, cols] + jnp.dot(vt, p, preferred_element_type=F32)
        return m_new

    def scores(t, buf1, buf2):
        kt = t + (t >= j).astype(jnp.int32)
        before = kt < j
        k = k_ref[pl.ds(pl.multiple_of(kt * tk, tk), tk), :]
        kc = jnp.concatenate([k, jnp.where(before, kaug, kaug_neg)], axis=1)
        s1 = lax.dot_general(kc, qc1, _NT, preferred_element_type=F32)
        s2 = lax.dot_general(kc, qc2, _NT, preferred_element_type=F32)
        buf1[...] = s1
        buf2[...] = s2
        cq = jnp.where(before, -slope, slope) * (rq + (j * tq - kt * tk).astype(F32))
        return kt, cq, jnp.max(s1, axis=0, keepdims=True), jnp.max(s2, axis=0, keepdims=True)

    def scores_diagonal(buf1, buf2):
        k = k_ref[pl.ds(pl.multiple_of(j * tk, tk), tk), :]
        dist = jnp.abs(lax.broadcasted_iota(jnp.int32, (tk, tq), 0)
                       - lax.broadcasted_iota(jnp.int32, (tk, tq), 1)).astype(F32)
        bias = slope * dist
        s1 = lax.dot_general(k, q1, _NT, preferred_element_type=F32) - bias
        s2 = lax.dot_general(k, q2, _NT, preferred_element_type=F32) - bias
        buf1[...] = s1
        buf2[...] = s2
        zer = jnp.zeros((1, tq), F32)
        return j, zer, jnp.max(s1, axis=0, keepdims=True), jnp.max(s2, axis=0, keepdims=True)

    ones = jnp.ones((ONES_ROWS, tk), BF16)

    def consume(tile, buf1, buf2, state):
        kt, cq, mx1, mx2 = tile
        m1, m2 = state
        n_slab = tk // vt_ref.shape[2]
        vt = jnp.concatenate([vt_ref[kt * n_slab + r] for r in range(n_slab)], axis=1)
        vt = jnp.concatenate([vt, ones], axis=0)
        return online(buf1, mx1, cq, m1, acc1, vt), online(buf2, mx2, cq, m2, acc2, vt)

    neg = jnp.full((1, tq), NEG_BIG, F32)
    state = (neg, neg)
    tile_a = scores_diagonal(sa1, sa2)
    n_pairs = (nk - 1) // 2

    def body(u, carry):
        state, tile_a = carry
        tile_b = scores(2 * u, sb1, sb2)
        state = consume(tile_a, sa1, sa2, state)
        tile_a = scores(2 * u + 1, sa1, sa2)
        state = consume(tile_b, sb1, sb2, state)
        return state, tile_a

    if n_pairs > 0:
        state, tile_a = lax.fori_loop(0, n_pairs, body, (state, tile_a))
    if nk % 2 == 0:
        tile_b = scores(nk - 2, sb1, sb2)
        state = consume(tile_a, sa1, sa2, state)
        consume(tile_b, sb1, sb2, state)
    else:
        consume(tile_a, sa1, sa2, state)

    lv = lam_ref[...]
    lam = (jnp.exp(jnp.sum(lv[0:1] * lv[1:2], axis=1, keepdims=True))
           - jnp.exp(jnp.sum(lv[2:3] * lv[3:4], axis=1, keepdims=True)) + lam_init)
    l1 = acc1[HEAD_DIM:HEAD_DIM + 1, :]
    l2 = acc2[HEAD_DIM:HEAD_DIM + 1, :]
    o = acc1[0:HEAD_DIM, :] / l1 - lam * (acc2[0:HEAD_DIM, :] / l2)
    ms = jnp.mean(o * o, axis=0, keepdims=True)
    o = o * lax.rsqrt(ms + RMS_EPS) * (1.0 - lam_init)
    o = o * g_ref[...]
    o_ref[...] = o.T.astype(BF16)


def _alibi_tables(slopes2, tk):
    assert tk <= 256 * 256
    r = jnp.arange(tk, dtype=jnp.int32)
    kaug = jnp.zeros((tk, LANES), F32)
    kaug = kaug.at[:, 0:3].set((r % 256).astype(F32)[:, None]).at[:, 3:6].set((r // 256).astype(F32)[:, None])
    s_a = slopes2.astype(BF16).astype(F32)
    s_b = (slopes2 - s_a).astype(BF16).astype(F32)
    s_c = (slopes2 - s_a - s_b).astype(BF16).astype(F32)
    pieces = jnp.stack([s_a, s_b, s_c], axis=1)
    tab = jnp.zeros((slopes2.shape[0], LANES), F32)
    tab = tab.at[:, 0:3].set(pieces).at[:, 3:6].set(256.0 * pieces)
    return tab, kaug.astype(BF16)


def _diff_attn(proj, vt_all, slopes2, tab, kaug, lam_l, g_col, *, row0, batch, s_len, tq, tk, lam_init):
    assert tq == tk
    slab = vt_all.shape[2]
    nq = s_len // tq
    qb0 = row0 // tq
    kb0 = row0 // s_len
    n_kh = DQ // HEAD_DIM
    return pl.pallas_call(
        functools.partial(_diff_attn_kernel, s_len=s_len, tq=tq, tk=tk, lam_init=lam_init),
        out_shape=jax.ShapeDtypeStruct((batch * s_len, DQ), BF16),
        grid=(batch, N_DIFF_HEADS, nq),
        in_specs=[
            pl.BlockSpec(memory_space=pltpu.SMEM),
            pl.BlockSpec(tab.shape, lambda b, h, j: (0, 0)),
            pl.BlockSpec((tq, HEAD_DIM), lambda b, h, j: (qb0 + b * nq + j, h)),
            pl.BlockSpec((s_len, HEAD_DIM), lambda b, h, j: (kb0 + b, n_kh + h)),
            pl.BlockSpec((tk, LANES), lambda b, h, j: (0, 0)),
            pl.BlockSpec((s_len // slab, HEAD_DIM, slab), lambda b, h, j: (kb0 + b, h, 0)),
            pl.BlockSpec((4, DIFF_HALF), lambda b, h, j: (0, 0)),
            pl.BlockSpec((HEAD_DIM, 1), lambda b, h, j: (0, 0)),
        ],
        out_specs=pl.BlockSpec((tq, HEAD_DIM), lambda b, h, j: (b * nq + j, h)),
        scratch_shapes=[pltpu.VMEM((HEAD_DIM + ONES_ROWS, tq), F32)] * 2 + [pltpu.VMEM((tk, tq), F32)] * 4,
        compiler_params=_cparams(3),
        name="diff_attn",
    )(slopes2, tab, proj, proj, kaug, vt_all, lam_l, g_col)


def _win_attn_kernel(sl_ref, sink_ref, q_ref, k_ref, v_ref, o_ref, *, s_len, tq):
    kv = pl.program_id(1)
    j = pl.program_id(2)
    w = tq + 2 * WINDOW
    ws = pl.multiple_of(jnp.clip(j * tq - WINDOW, 0, s_len - w), WINDOW)
    k = k_ref[pl.ds(ws, w), :]
    v = v_ref[pl.ds(ws, w), :]
    qpos = j * tq + lax.broadcasted_iota(jnp.int32, (tq, w), 0)
    kpos = ws + lax.broadcasted_iota(jnp.int32, (tq, w), 1)
    rel = jnp.abs(qpos - kpos)
    relf = jnp.where(rel <= WINDOW, rel.astype(F32), FAR)
    v_ones = jnp.concatenate([v, jnp.ones_like(v)], axis=1)
    for g in range(WIN_GROUP):
        hidx = kv * WIN_GROUP + g
        qg = q_ref[:, g * HEAD_DIM:(g + 1) * HEAD_DIM]
        s = lax.dot_general(qg, k, _NT, preferred_element_type=F32) - sl_ref[hidx] * relf
        sk = sink_ref[hidx]
        m = jnp.maximum(jnp.max(s, axis=1, keepdims=True), sk)
        p = jnp.exp2(s - m).astype(BF16)
        ov = jnp.dot(p, v_ones, preferred_element_type=F32)
        den = ov[:, HEAD_DIM:] + jnp.exp2(sk - m)
        o_ref[:, g * HEAD_DIM:(g + 1) * HEAD_DIM] = (ov[:, :HEAD_DIM] / den).astype(BF16)


def _win_attn(proj, slopes2, sink2, *, row0, batch, s_len, tq):
    nq = s_len // tq
    qb0 = row0 // tq
    kb0 = row0 // s_len
    gw = WIN_GROUP * HEAD_DIM
    q_col0 = (2 * DQ) // gw
    k_col0 = (2 * DQ + WQ) // HEAD_DIM
    v_col0 = (2 * DQ + WQ + WKV) // HEAD_DIM
    return pl.pallas_call(
        functools.partial(_win_attn_kernel, s_len=s_len, tq=tq),
        out_shape=jax.ShapeDtypeStruct((batch * s_len, WQ), BF16),
        grid=(batch, N_WIN_KV, nq),
        in_specs=[
            pl.BlockSpec(memory_space=pltpu.SMEM),
            pl.BlockSpec(memory_space=pltpu.SMEM),
            pl.BlockSpec((tq, gw), lambda b, kv, j: (qb0 + b * nq + j, q_col0 + kv)),
            pl.BlockSpec((s_len, HEAD_DIM), lambda b, kv, j: (kb0 + b, k_col0 + kv)),
            pl.BlockSpec((s_len, HEAD_DIM), lambda b, kv, j: (kb0 + b, v_col0 + kv)),
        ],
        out_specs=pl.BlockSpec((tq, gw), lambda b, kv, j: (b * nq + j, kv)),
        compiler_params=_cparams(3),
        name="win_attn",
    )(slopes2, sink2, proj, proj, proj)


def _pack_rows(v):
    half = v.shape[1] // 2
    lo = lax.bitcast_convert_type(v[:, :half].astype(BF16).astype(F32), jnp.uint32)
    hi = lax.bitcast_convert_type(v[:, half:].astype(BF16).astype(F32), jnp.uint32)
    return (lo >> 16) | hi


def _unpack_rows(u):
    lo = lax.bitcast_convert_type(u << 16, F32)
    hi = lax.bitcast_convert_type(u & jnp.uint32(0xFFFF0000), F32)
    return lo, hi


def _store_token_major(ref, u):
    rows, width = u.shape
    p = width // LANES
    for c in range(p):
        ref[pl.ds(c, rows, stride=p), :] = u[:, c * LANES:(c + 1) * LANES]


def _load_token_major(ref, rows):
    p = ref.shape[0] // rows
    return jnp.concatenate([ref[pl.ds(c, rows, stride=p), :] for c in range(p)], axis=1)


def _layer_norm(z, g, b):
    mu = jnp.mean(z, axis=1, keepdims=True)
    zc = z - mu
    var = jnp.mean(zc * zc, axis=1, keepdims=True)
    return zc * lax.rsqrt(var + LN_EPS) * g + b


def _outproj_kernel(hda_ref, hdb_ref, hwa_ref, hwb_ref, xa_ref, xb_ref, wod_ref, wow_ref,
                    g1_ref, lng_ref, lnb_ref, sc_ref, sh_ref, wr2_ref, wrh_ref, br_ref,
                    x1_ref, h2_ref, er_ref, w0_ref, w1_ref, cnt_ref, *, npb, alpha, tm, n_chunks):
    i = pl.program_id(0)
    first = i < npb
    rows = tm // n_chunks
    p = h2_ref.shape[0] // tm

    @pl.when(i == 0)
    def _():
        cnt_ref[...] = jnp.zeros_like(cnt_ref)

    iota8 = lax.broadcasted_iota(jnp.int32, (SUBLANES, rows), 0)
    iota_e = lax.broadcasted_iota(jnp.int32, (N_EXPERTS, rows), 0)
    upper = jnp.where(lax.broadcasted_iota(jnp.int32, (rows, rows), 0)
                      < lax.broadcasted_iota(jnp.int32, (rows, rows), 1), 1.0, 0.0).astype(BF16)

    def first_argmax(v):
        vmax = jnp.max(v, axis=0, keepdims=True)
        idx = jnp.min(jnp.where(v == vmax, iota8, SUBLANES), axis=0, keepdims=True)
        return vmax, idx

    for c in range(n_chunks):
        rs = slice(c * rows, (c + 1) * rows)
        x = jnp.where(first, xa_ref[rs, :], xb_ref[rs, :])
        hd = jnp.where(first, hda_ref[rs, :], hdb_ref[rs, :])
        hw = jnp.where(first, hwa_ref[rs, :], hwb_ref[rs, :])
        att = (jnp.dot(hd, wod_ref[...], preferred_element_type=F32)
               + jnp.dot(hw, wow_ref[...], preferred_element_type=F32))
        x1 = _layer_norm(alpha * x + g1_ref[...] * att, lng_ref[...], lnb_ref[...])
        x1_ref[rs, :] = x1
        h2 = x1 * (1.0 + sc_ref[...]) + sh_ref[...]
        _store_token_major(h2_ref.at[pl.ds(c * rows * p, rows * p)], _pack_rows(h2))

        h_hi = h2.astype(BF16)
        h_lo = (h2 - h_hi.astype(F32)).astype(BF16)
        both = jnp.dot(h_hi, wr2_ref[...], preferred_element_type=F32)
        lt = (both[:, :LANES] + both[:, LANES:]
              + jnp.dot(h_lo, wrh_ref[...], preferred_element_type=F32))
        lt = lt.T + br_ref[...]

        gl = lt[0:SUBLANES]
        gmax, g_idx = first_argmax(gl)
        g_w = 1.0 / jnp.sum(jnp.exp(gl - gmax), axis=0, keepdims=True)
        el = jnp.zeros((SUBLANES, rows), F32)
        for g in range(N_GROUPS):
            lo = SUBLANES + g * EXPERTS_PER_GROUP
            el = jnp.where(g_idx == g, lt[lo:lo + EXPERTS_PER_GROUP], el)
        v0, i0 = first_argmax(el)
        el2 = jnp.where(iota8 == i0, -jnp.inf, el)
        v1, i1 = first_argmax(el2)
        t = jnp.exp(v1 - v0)
        w0 = g_w / (1.0 + t)
        w1 = g_w * t / (1.0 + t)
        e0 = g_idx * EXPERTS_PER_GROUP + i0
        e1 = g_idx * EXPERTS_PER_GROUP + i1

        hit0 = iota_e == e0
        hit1 = iota_e == e1
        cnt = jnp.where(hit0, 1.0, jnp.where(hit1, 1.0, 0.0))
        before = jnp.dot(cnt.astype(BF16), upper, preferred_element_type=F32)
        tot = before + cnt_ref[:, 0:1]
        r0 = jnp.sum(jnp.where(hit0, tot, 0.0), axis=0, keepdims=True).astype(jnp.int32)
        r1 = jnp.sum(jnp.where(hit1, tot, 0.0), axis=0, keepdims=True).astype(jnp.int32)
        cnt_ref[...] = cnt_ref[...] + jnp.sum(cnt, axis=1, keepdims=True)

        er_ref[:, rs] = jnp.where(iota8 == 0, e0, jnp.where(iota8 == 1, e1,
                                  jnp.where(iota8 == 2, r0, jnp.where(iota8 == 3, r1, 0))))
        w0_ref[rs, :] = jnp.broadcast_to(w0, (LANES, rows)).T
        w1_ref[rs, :] = jnp.broadcast_to(w1, (LANES, rows)).T


def _outproj(lay, hd, hw, xa, xb, same_array, mod3, wo_d, wo_w, ln_g, ln_b, wr_hi, wr_lo, br, alpha, tm):
    d = xa.shape[1]
    p = d // (2 * LANES)
    const = dict(pipeline_mode=pl.Buffered(1))
    vec = lambda: pl.BlockSpec((1, d), lambda i: (0, 0))
    return pl.pallas_call(
        functools.partial(_outproj_kernel, npb=lay.tp // tm, alpha=alpha, tm=tm,
                          n_chunks=OUTPROJ_CHUNKS),
        out_shape=(jax.ShapeDtypeStruct((lay.t, d), F32),
                   jax.ShapeDtypeStruct((lay.t * p, LANES), jnp.uint32),
                   jax.ShapeDtypeStruct((SUBLANES, lay.t), jnp.int32),
                   jax.ShapeDtypeStruct((lay.t, LANES), F32),
                   jax.ShapeDtypeStruct((lay.t, LANES), F32),
                   jax.ShapeDtypeStruct((N_EXPERTS, LANES), F32)),
        grid=(lay.t // tm,),
        in_specs=_x_specs(lay, tm, DQ, False) + _x_specs(lay, tm, WQ, False)
        + _x_specs(lay, tm, d, same_array) + [
            pl.BlockSpec((DQ, d), lambda i: (0, 0), **const),
            pl.BlockSpec((WQ, d), lambda i: (0, 0), **const),
            _mod_spec(lay, tm, 2, d),
            vec(), vec(),
            _mod_spec(lay, tm, 4, d),
            _mod_spec(lay, tm, 3, d),
            pl.BlockSpec((d, 2 * LANES), lambda i: (0, 0)),
            pl.BlockSpec((d, LANES), lambda i: (0, 0)),
            pl.BlockSpec((LANES, 1), lambda i: (0, 0)),
        ],
        out_specs=(pl.BlockSpec((tm, d), lambda i: (i, 0)),
                   pl.BlockSpec((tm * p, LANES), lambda i: (i, 0)),
                   pl.BlockSpec((SUBLANES, tm), lambda i: (0, i)),
                   pl.BlockSpec((tm, LANES), lambda i: (i, 0)),
                   pl.BlockSpec((tm, LANES), lambda i: (i, 0)),
                   pl.BlockSpec((N_EXPERTS, LANES), lambda i: (0, 0))),
        compiler_params=_cparams(1),
        name="out_proj_ln_router",
    )(*hd, *hw, xa, xb, wo_d, wo_w, mod3, ln_g, ln_b, mod3, mod3,
      jnp.concatenate([wr_hi, wr_lo], axis=1), wr_hi, br)


def _token_copy(src, s_tok, dst, d_tok, sem, p):
    return pltpu.make_async_copy(src.at[pl.ds(pl.multiple_of(s_tok * p, p), p)],
                                 dst.at[pl.ds(pl.multiple_of(d_tok * p, p), p)], sem)


def _dispatch_kernel(zrow_ref, pos_ref, h_ref, xs_ref, zbuf, sem, zsem, *, ct, tm, p):
    i = pl.program_id(0)

    def zero_copy(row):
        start = pl.multiple_of(row * p, tm * p)
        return pltpu.make_async_copy(zbuf, xs_ref.at[pl.ds(start, tm * p)], zsem)

    @pl.when(i == 0)
    def _():
        zbuf[...] = jnp.zeros_like(zbuf)
        n_tiles = xs_ref.shape[0] // (tm * p)
        for e in range(N_EXPERTS):
            @pl.when(zrow_ref[e] >= 0)
            def _():
                zero_copy(zrow_ref[e]).start()
        lax.fori_loop(zrow_ref[N_EXPERTS], n_tiles, lambda k, c: (zero_copy(k * tm).start(), c)[1], 0)
        for e in range(N_EXPERTS):
            @pl.when(zrow_ref[e] >= 0)
            def _():
                zero_copy(zrow_ref[e]).wait()
        lax.fori_loop(zrow_ref[N_EXPERTS], n_tiles, lambda k, c: (zero_copy(k * tm).wait(), c)[1], 0)

    def body(j, carry):
        _token_copy(h_ref, j, xs_ref, pos_ref[2 * j], sem, p).start(priority=0)
        _token_copy(h_ref, j, xs_ref, pos_ref[2 * j + 1], sem, p).start(priority=1)
        return carry

    lax.fori_loop(0, ct, body, 0, unroll=8)
    for _ in range(2):
        pltpu.make_async_copy(h_ref, xs_ref.at[pl.ds(0, ct * p)], sem).wait()


def _dispatch(zrow, pos_flat, h2, n_rows, ct, tm, p):
    t = h2.shape[0] // p
    return pl.pallas_call(
        functools.partial(_dispatch_kernel, ct=ct, tm=tm, p=p),
        out_shape=jax.ShapeDtypeStruct((n_rows * p, LANES), jnp.uint32),
        grid=(t // ct,),
        in_specs=[
            pl.BlockSpec(memory_space=pltpu.SMEM),
            pl.BlockSpec((2 * ct,), lambda i: (i,), memory_space=pltpu.SMEM),
            pl.BlockSpec((ct * p, LANES), lambda i: (i, 0)),
        ],
        out_specs=pl.BlockSpec(memory_space=pl.ANY),
        scratch_shapes=[pltpu.VMEM((tm * p, LANES), jnp.uint32),
                        pltpu.SemaphoreType.DMA, pltpu.SemaphoreType.DMA],
        compiler_params=_cparams(1),
        name="moe_dispatch",
    )(zrow, pos_flat, h2)


def _experts_kernel(te_ref, nu_ref, x_ref, wg_ref, wu_ref, wd_ref, o_ref, wg_b, wu_b, wd_b, *, tm):
    i = pl.program_id(0)

    @pl.when(i < nu_ref[0])
    def _():
        @pl.when((i == 0) | (te_ref[i] != te_ref[jnp.maximum(i - 1, 0)]))
        def _():
            wg_b[...] = wg_ref[...].astype(BF16)
            wu_b[...] = wu_ref[...].astype(BF16)
            wd_b[...] = wd_ref[...].astype(BF16)

        lo, hi = _unpack_rows(_load_token_major(x_ref, tm))
        x = jnp.concatenate([lo.astype(BF16), hi.astype(BF16)], axis=1)
        g = jnp.dot(x, wg_b[...], preferred_element_type=F32)
        u = jnp.dot(x, wu_b[...], preferred_element_type=F32)
        a = (g * jax.nn.sigmoid(g) * u).astype(BF16)
        y = jnp.dot(a, wd_b[...], preferred_element_type=F32)
        _store_token_major(o_ref, _pack_rows(y))

    @pl.when(pl.program_id(0) >= nu_ref[0])
    def _():
        o_ref[...] = jnp.zeros_like(o_ref)


def _experts(tile_expert, n_used, xs, wg, wu, wd, layer, tm, p):
    d, f = wg.shape[2], wg.shape[3]
    n_rows = xs.shape[0] // p
    n_tiles = n_rows // tm

    def tile(i, te, nu):
        return jnp.minimum(i, nu[0] - 1)

    grid_spec = pltpu.PrefetchScalarGridSpec(
        num_scalar_prefetch=2,
        grid=(n_tiles,),
        in_specs=[
            pl.BlockSpec((tm * p, LANES), lambda i, te, nu: (tile(i, te, nu), 0)),
            pl.BlockSpec((None, None, d, f), lambda i, te, nu: (layer, te[tile(i, te, nu)], 0, 0)),
            pl.BlockSpec((None, None, d, f), lambda i, te, nu: (layer, te[tile(i, te, nu)], 0, 0)),
            pl.BlockSpec((None, None, f, d), lambda i, te, nu: (layer, te[tile(i, te, nu)], 0, 0)),
        ],
        out_specs=pl.BlockSpec((tm * p, LANES), lambda i, te, nu: (i, 0)),
        scratch_shapes=[pltpu.VMEM((d, f), BF16), pltpu.VMEM((d, f), BF16), pltpu.VMEM((f, d), BF16)],
    )
    return pl.pallas_call(
        functools.partial(_experts_kernel, tm=tm),
        out_shape=jax.ShapeDtypeStruct((n_rows * p, LANES), jnp.uint32),
        grid_spec=grid_spec,
        compiler_params=_cparams(1),
        name="moe_experts",
    )(tile_expert, n_used, xs, wg, wu, wd)


def _final_kernel(posc_ref, posn_ref, x1_ref, ys_ref, w0_ref, w1_ref, g2_ref, lng_ref, lnb_ref, o_ref,
                  ya0, yb0, ya1, yb1, sems, *, alpha, tm, p, n_steps):
    i = pl.program_id(0)
    bufs = ((ya0, yb0), (ya1, yb1))

    def gather(pos_ref, s):
        def body(j, carry):
            _token_copy(ys_ref, pos_ref[2 * j], bufs[s][0], j, sems.at[0, s], p).start(priority=0)
            _token_copy(ys_ref, pos_ref[2 * j + 1], bufs[s][1], j, sems.at[1, s], p).start(priority=1)
            return carry
        lax.fori_loop(0, tm, body, 0, unroll=8)

    def wait(s):
        for k in range(2):
            pltpu.make_async_copy(ys_ref.at[pl.ds(0, tm * p)], bufs[s][k], sems.at[k, s]).wait()

    @pl.when(i == 0)
    def _():
        gather(posc_ref, 0)

    reps = x1_ref.shape[1] // LANES
    for s in range(2):
        @pl.when(i % 2 == s)
        def _():
            @pl.when(i + 1 < n_steps)
            def _():
                gather(posn_ref, 1 - s)
            wait(s)
            w0 = jnp.concatenate([w0_ref[...]] * reps, axis=1)
            w1 = jnp.concatenate([w1_ref[...]] * reps, axis=1)
            ya = jnp.concatenate(_unpack_rows(_load_token_major(bufs[s][0], tm)), axis=1)
            yb = jnp.concatenate(_unpack_rows(_load_token_major(bufs[s][1], tm)), axis=1)
            y = w0 * ya + w1 * yb
            o_ref[...] = _layer_norm(alpha * x1_ref[...] + g2_ref[...] * y, lng_ref[...], lnb_ref[...])


def _final(lay, pos_flat, x1, ys, w0, w1, mod3, ln_g, ln_b, alpha, tm, p, row0, n_rows):
    d = x1.shape[1]
    b0 = row0 // tm
    n_steps = n_rows // tm
    last = lay.t // tm - 1
    row = lambda i: (b0 + i, 0)
    vec = lambda: pl.BlockSpec((1, d), lambda i: (0, 0))
    slab = pltpu.VMEM((tm * p, LANES), jnp.uint32)
    return pl.pallas_call(
        functools.partial(_final_kernel, alpha=alpha, tm=tm, p=p, n_steps=n_steps),
        out_shape=jax.ShapeDtypeStruct((n_rows, d), F32),
        grid=(n_steps,),
        in_specs=[
            pl.BlockSpec((2 * tm,), lambda i: (b0 + i,), memory_space=pltpu.SMEM),
            pl.BlockSpec((2 * tm,), lambda i: (jnp.minimum(b0 + i + 1, last),), memory_space=pltpu.SMEM),
            pl.BlockSpec((tm, d), row),
            pl.BlockSpec(memory_space=pl.ANY),
            pl.BlockSpec((tm, LANES), row), pl.BlockSpec((tm, LANES), row),
            pl.BlockSpec((None, 1, d), lambda i: (lay.batch_of_tile(b0 + i, tm), 0, 5)),
            vec(), vec(),
        ],
        out_specs=pl.BlockSpec((tm, d), lambda i: (i, 0)),
        scratch_shapes=[slab, slab, slab, slab, pltpu.SemaphoreType.DMA((2, 2))],
        compiler_params=_cparams(1),
        name="moe_combine_ln",
    )(pos_flat, pos_flat, x1, ys, w0, w1, mod3, ln_g, ln_b)


def _routing_tables(er, cnt, tm, n_tiles):
    counts = cnt[:, 0].astype(jnp.int32)
    padded = ((counts + tm - 1) // tm) * tm
    ends = jnp.cumsum(padded)
    offs = ends - padded
    pos0 = offs[er[0]] + er[2]
    pos1 = offs[er[1]] + er[3]
    pos_flat = jnp.stack([pos0, pos1], axis=1).reshape(-1)
    tile_start = jnp.arange(n_tiles, dtype=jnp.int32) * tm
    tile_expert = jnp.minimum(
        jnp.sum((tile_start[:, None] >= ends[None, :]).astype(jnp.int32), axis=1), N_EXPERTS - 1)
    n_used = (ends[-1] // tm).reshape(1)
    zrow = jnp.concatenate([jnp.where(counts > 0, ends - tm, -1), n_used])
    return pos_flat, tile_expert.astype(jnp.int32), n_used.astype(jnp.int32), zrow.astype(jnp.int32)


def kernel(x_prompt, x_sample, c_prompt, c_sample, w_ada, b_ada, w_in, lam, subln_g, sink,
           w_o, ln_g, ln_b, w_rg, b_rg, w_re, b_re, w_gate, w_up, w_down):
    bp, sp, d = x_prompt.shape
    bs, ss, _ = x_sample.shape
    depth = w_ada.shape[0]
    lay = _Layout(bp, sp, bs, ss)
    alpha = (2.0 * depth) ** 0.25
    assert d % (2 * LANES * SUBLANES) == 0 and lay.tp % ss == 0
    p = d // (2 * LANES)

    tm = _pick(math.gcd(sp, ss), 512)
    tq_d = tk_d = _pick(math.gcd(sp, ss), DIFF_TILE)
    tq_w = _pick(math.gcd(sp, ss), 512)
    while tq_w + 2 * WINDOW > min(sp, ss):
        tq_w //= 2
    ct = _pick(lay.t, 1024)
    n_tiles = (2 * lay.t + N_EXPERTS * (tm - 1) + tm - 1) // tm
    n_rows = n_tiles * tm

    xa = x_prompt.reshape(lay.tp, d)
    xb = x_sample.reshape(bs * ss, d)

    nb = bp + bs
    c_pad = jnp.zeros((-(-nb // SUBLANES) * SUBLANES, d), F32)
    c_pad = c_pad.at[:bp].set(c_prompt).at[bp:nb].set(c_sample)
    mod = _adaln(c_pad, w_ada, b_ada)

    slopes_d = jnp.asarray([LOG2E * 2.0 ** (-8.0 * (h + 1) / N_DIFF_HEADS) for h in range(N_DIFF_HEADS)], F32)
    slopes_w = jnp.asarray([LOG2E * 2.0 ** (-8.0 * (h + 1) / N_WIN_HEADS) for h in range(N_WIN_HEADS)], F32)
    tab_d, kaug_d = _alibi_tables(slopes_d, tk_d)
    cs = jnp.ones((2 * DQ + WQ + 2 * WKV,), F32)
    cs = cs.at[:DQ].set(LOG2E * DIFF_HALF ** -0.5).at[2 * DQ:2 * DQ + WQ].set(LOG2E * HEAD_DIM ** -0.5)
    cs = cs.reshape(1, -1)

    same = False
    x_last = None
    for l in range(depth):
        lam_init = 0.8 - 0.6 * math.exp(-0.3 * l)
        mod3 = mod[l].reshape(-1, 1, 6 * d)
        wl = w_in[l]
        w_main = jnp.concatenate([wl[:, :2 * DQ], wl[:, 3 * DQ:]], axis=1).astype(BF16)
        wvt = wl[:, 2 * DQ:3 * DQ].T.astype(BF16)
        proj, vt_all = _inproj(lay, xa, xb, same, mod3, w_main, wvt, cs, tm)

        g_col = subln_g[l].reshape(HEAD_DIM, 1)
        groups = (dict(row0=0, batch=bp, s_len=sp), dict(row0=lay.tp, batch=bs, s_len=ss))
        hd = [_diff_attn(proj, vt_all, slopes_d, tab_d, kaug_d, lam[l], g_col, lam_init=lam_init,
                         tq=tq_d, tk=tk_d, **g) for g in groups]
        sink2 = sink[l].astype(F32) * LOG2E
        hw = [_win_attn(proj, slopes_w, sink2, tq=tq_w, **g) for g in groups]

        wr = jnp.zeros((d, LANES), F32)
        wr = wr.at[:, :N_GROUPS].set(w_rg[l])
        wr = wr.at[:, SUBLANES:SUBLANES + N_EXPERTS].set(
            jnp.transpose(w_re[l], (1, 0, 2)).reshape(d, N_EXPERTS))
        wr_hi = wr.astype(BF16)
        wr_lo = (wr - wr_hi.astype(F32)).astype(BF16)
        br = jnp.full((LANES,), NEG_BIG, F32)
        br = br.at[:N_GROUPS].set(b_rg[l]).at[SUBLANES:SUBLANES + N_EXPERTS].set(b_re[l].reshape(-1))
        br = br.reshape(LANES, 1)

        wo = w_o[l].astype(BF16)
        x1, h2, er, w0, w1, cnt = _outproj(
            lay, hd, hw, xa, xb, same, mod3, wo[:DQ], wo[DQ:], ln_g[l, 0].reshape(1, d),
            ln_b[l, 0].reshape(1, d), wr_hi, wr_lo, br, alpha, tm)

        pos_flat, tile_expert, n_used, zrow = _routing_tables(er, cnt, tm, n_tiles)
        xs = _dispatch(zrow, pos_flat, h2, n_rows, ct, tm, p)
        ys = _experts(tile_expert, n_used, xs, w_gate, w_up, w_down, l, tm, p)
        fin = functools.partial(_final, lay, pos_flat, x1, ys, w0, w1, mod3, ln_g[l, 1].reshape(1, d),
                                ln_b[l, 1].reshape(1, d), alpha, tm, p)
        if l + 1 < depth:
            x_all = fin(0, lay.t)
            xa = xb = x_all
            same = True
        else:
            x_last = (fin(0, lay.tp), fin(lay.tp, lay.t - lay.tp))

    return (x_last[0].reshape(bp, sp, d), x_last[1].reshape(bs, ss, d))
```

```python
import functools
import math

import jax
import jax.numpy as jnp
from jax import lax
from jax.experimental import pallas as pl
from jax.experimental.pallas import tpu as pltpu

F32 = jnp.float32
BF16 = jnp.bfloat16

HEAD_DIM = 128
DIFF_HALF = HEAD_DIM // 2
N_DIFF_HEADS = 8
N_WIN_HEADS = 8
N_WIN_KV = 2
WIN_GROUP = N_WIN_HEADS // N_WIN_KV
WINDOW = 128
DQ = N_DIFF_HEADS * HEAD_DIM
WQ = N_WIN_HEADS * HEAD_DIM
WKV = N_WIN_KV * HEAD_DIM
N_GROUPS = 4
EXPERTS_PER_GROUP = 8
N_EXPERTS = N_GROUPS * EXPERTS_PER_GROUP
LN_EPS = 1e-5
RMS_EPS = 1e-5
LOG2E = 1.4426950408889634
NEG_BIG = -1e30
FAR = 1e30

V7X_VMEM_BYTES = 64 * 1024 * 1024
VMEM_LIMIT = V7X_VMEM_BYTES - 8 * 1024 * 1024
LANES = 128
SUBLANES = 8
ROW_TILE = 512
WIN_TILE = 512
DIFF_TILE = 1024
ADALN_COLS = 1024
DISPATCH_TOKENS = 1024
BF16_INT_RANGE = 256
OUTPROJ_CHUNKS = 2
ONES_ROWS = 16
SUM_ON_MXU_BELOW_TILES = 4

_NT = (((1,), (1,)), ((), ()))


def _cparams(n_axes):
    return pltpu.CompilerParams(
        dimension_semantics=("arbitrary",) * n_axes, vmem_limit_bytes=VMEM_LIMIT)


def _pick(n, pref):
    t = min(pref, n)
    while n % t:
        t //= 2
    return t


def _adaln_kernel(c_ref, w_ref, b_ref, o_ref):
    c = c_ref[...]
    s = (c * jax.nn.sigmoid(c)).astype(BF16)
    o_ref[...] = jnp.dot(s, w_ref[...].astype(BF16), preferred_element_type=F32) + b_ref[...]


def _adaln(c_pad, w_ada, b_ada):
    depth, d, n = w_ada.shape
    tn = _pick(n, ADALN_COLS)
    rows = c_pad.shape[0]
    return pl.pallas_call(
        _adaln_kernel,
        out_shape=jax.ShapeDtypeStruct((depth, rows, n), F32),
        grid=(depth, n // tn),
        in_specs=[
            pl.BlockSpec((rows, d), lambda l, j: (0, 0)),
            pl.BlockSpec((None, d, tn), lambda l, j: (l, 0, j)),
            pl.BlockSpec((None, 1, tn), lambda l, j: (l, 0, j)),
        ],
        out_specs=pl.BlockSpec((None, rows, tn), lambda l, j: (l, 0, j)),
        compiler_params=_cparams(2),
        name="adaln_mod",
    )(c_pad, w_ada, b_ada.reshape(depth, 1, n))


class _Layout:
    def __init__(self, bp, sp, bs, ss):
        self.bp, self.sp, self.bs, self.ss = bp, sp, bs, ss
        self.tp = bp * sp
        self.t = self.tp + bs * ss

    def batch_of_tile(self, i, tm):
        row = i * tm
        return jnp.where(row < self.tp, row // self.sp, self.bp + (row - self.tp) // self.ss)


def _mod_spec(lay, tm, chunk, d):
    return pl.BlockSpec((None, 1, d), lambda i: (lay.batch_of_tile(i, tm), 0, chunk))


def _x_specs(lay, tm, d, same_array):
    npb = lay.tp // tm
    boff = npb if same_array else 0
    return [
        pl.BlockSpec((tm, d), lambda i: (jnp.minimum(i, npb - 1), 0)),
        pl.BlockSpec((tm, d), lambda i: (jnp.maximum(i - npb, 0) + boff, 0)),
    ]


def _select_x(i, npb, xa_ref, xb_ref):
    return jnp.where(i < npb, xa_ref[...], xb_ref[...])


def _inproj_kernel(xa_ref, xb_ref, sc_ref, sh_ref, w_ref, wvt_ref, cs_ref, proj_ref, vt_ref, *, npb):
    x = _select_x(pl.program_id(0), npb, xa_ref, xb_ref)
    h = (x * (1.0 + sc_ref[...]) + sh_ref[...]).astype(BF16)
    acc = jnp.dot(h, w_ref[...], preferred_element_type=F32)
    proj_ref[...] = (acc * cs_ref[...]).astype(BF16)
    vt = lax.dot_general(wvt_ref[...], h, _NT, preferred_element_type=F32)
    vt_ref[...] = vt.astype(BF16)


def _inproj(lay, xa, xb, same_array, mod3, w_main, wvt, col_scale, tm):
    d = xa.shape[1]
    n = w_main.shape[1]
    nv = wvt.shape[0]
    const = dict(pipeline_mode=pl.Buffered(1))
    return pl.pallas_call(
        functools.partial(_inproj_kernel, npb=lay.tp // tm),
        out_shape=(jax.ShapeDtypeStruct((lay.t, n), BF16),
                   jax.ShapeDtypeStruct((lay.t // tm, nv, tm), BF16)),
        grid=(lay.t // tm,),
        in_specs=_x_specs(lay, tm, d, same_array) + [
            _mod_spec(lay, tm, 1, d),
            _mod_spec(lay, tm, 0, d),
            pl.BlockSpec((d, n), lambda i: (0, 0), **const),
            pl.BlockSpec((nv, d), lambda i: (0, 0), **const),
            pl.BlockSpec((1, n), lambda i: (0, 0)),
        ],
        out_specs=(pl.BlockSpec((tm, n), lambda i: (i, 0)),
                   pl.BlockSpec((None, nv, tm), lambda i: (i, 0, 0))),
        compiler_params=_cparams(1),
        name="in_proj",
    )(xa, xb, mod3, mod3, w_main, wvt, col_scale)


def _diff_attn_kernel(sl_ref, tab_ref, q_ref, k_ref, kaug_ref, dist_ref, vt_ref, lam_ref, g_ref, o_ref,
                      acc1, acc2, sa1, sa2, sb1, sb2, *, s_len, tq, tk, lam_init):
    nk = s_len // tk
    sum_on_mxu = acc1.shape[0] > HEAD_DIM
    h = pl.program_id(1)
    j = pl.program_id(2)
    q = q_ref[...]
    lane = lax.broadcasted_iota(jnp.int32, q.shape, 1)
    zero = jnp.zeros_like(q)
    q1 = jnp.where(lane < DIFF_HALF, q, zero)
    q2 = jnp.where(lane >= DIFF_HALF, q, zero)
    slope = sl_ref[h]
    aug = jnp.broadcast_to(tab_ref[pl.ds(h, 1), :], q.shape).astype(BF16)
    qc1 = jnp.concatenate([q1, aug], axis=1)
    qc2 = jnp.concatenate([q2, aug], axis=1)
    kaug = kaug_ref[...]
    kaug_neg = -kaug
    rq = lax.broadcasted_iota(jnp.int32, (1, tq), 1).astype(F32)
    acc1[...] = jnp.zeros_like(acc1)
    acc2[...] = jnp.zeros_like(acc2)

    half = _pick(tq, 2 * LANES)

    def online(s_ref, mx, cq, ml, acc, vt):
        m, l = ml
        m_new = jnp.maximum(m, mx + cq)
        a = jnp.exp2(m - m_new)
        shift = m_new - cq
        sums = []
        for c in range(tq // half):
            cols = slice(c * half, (c + 1) * half)
            p = jnp.exp2(s_ref[:, cols] - shift[:, cols])
            if not sum_on_mxu:
                sums.append(jnp.sum(p, axis=0, keepdims=True))
            acc[:, cols] = (a[:, cols] * acc[:, cols]
                            + jnp.dot(vt, p.astype(BF16), preferred_element_type=F32))
        return m_new, (l if sum_on_mxu else a * l + jnp.concatenate(sums, axis=1))

    def scores(t, buf1, buf2):
        kt = t + (t >= j).astype(jnp.int32)
        before = kt < j
        k = k_ref[pl.ds(pl.multiple_of(kt * tk, tk), tk), :]
        kc = jnp.concatenate([k, jnp.where(before, kaug, kaug_neg)], axis=1)
        s1 = lax.dot_general(kc, qc1, _NT, preferred_element_type=F32)
        s2 = lax.dot_general(kc, qc2, _NT, preferred_element_type=F32)
        buf1[...] = s1
        buf2[...] = s2
        cq = jnp.where(before, -slope, slope) * (rq + (j * tq - kt * tk).astype(F32))
        return kt, cq, jnp.max(s1, axis=0, keepdims=True), jnp.max(s2, axis=0, keepdims=True)

    def scores_diagonal(buf1, buf2):
        k = k_ref[pl.ds(pl.multiple_of(j * tk, tk), tk), :]
        bias = slope * dist_ref[...]
        s1 = lax.dot_general(k, q1, _NT, preferred_element_type=F32) - bias
        s2 = lax.dot_general(k, q2, _NT, preferred_element_type=F32) - bias
        buf1[...] = s1
        buf2[...] = s2
        zer = jnp.zeros((1, tq), F32)
        return j, zer, jnp.max(s1, axis=0, keepdims=True), jnp.max(s2, axis=0, keepdims=True)

    ones = jnp.ones((ONES_ROWS, tk), BF16)

    def consume(tile, buf1, buf2, state):
        kt, cq, mx1, mx2 = tile
        m1, m2 = state
        n_slab = tk // vt_ref.shape[2]
        vt = jnp.concatenate([vt_ref[kt * n_slab + r] for r in range(n_slab)], axis=1)
        if sum_on_mxu:
            vt = jnp.concatenate([vt, ones], axis=0)
        return online(buf1, mx1, cq, m1, acc1, vt), online(buf2, mx2, cq, m2, acc2, vt)

    neg = jnp.full((1, tq), NEG_BIG, F32)
    zer = jnp.zeros((1, tq), F32)
    state = ((neg, zer), (neg, zer))
    tile_a = scores_diagonal(sa1, sa2)
    n_pairs = (nk - 1) // 2

    def body(u, carry):
        state, tile_a = carry
        tile_b = scores(2 * u, sb1, sb2)
        state = consume(tile_a, sa1, sa2, state)
        tile_a = scores(2 * u + 1, sa1, sa2)
        state = consume(tile_b, sb1, sb2, state)
        return state, tile_a

    if n_pairs > 0:
        state, tile_a = lax.fori_loop(0, n_pairs, body, (state, tile_a))
    if nk % 2 == 0:
        tile_b = scores(nk - 2, sb1, sb2)
        state = consume(tile_a, sa1, sa2, state)
        state = consume(tile_b, sb1, sb2, state)
    else:
        state = consume(tile_a, sa1, sa2, state)
    (_, l1), (_, l2) = state

    lv = lam_ref[...]
    lam = (jnp.exp(jnp.sum(lv[0:1] * lv[1:2], axis=1, keepdims=True))
           - jnp.exp(jnp.sum(lv[2:3] * lv[3:4], axis=1, keepdims=True)) + lam_init)
    if sum_on_mxu:
        l1 = acc1[HEAD_DIM:HEAD_DIM + 1, :]
        l2 = acc2[HEAD_DIM:HEAD_DIM + 1, :]
    o = acc1[0:HEAD_DIM, :] / l1 - lam * (acc2[0:HEAD_DIM, :] / l2)
    ms = jnp.mean(o * o, axis=0, keepdims=True)
    o = o * lax.rsqrt(ms + RMS_EPS) * (1.0 - lam_init)
    o = o * g_ref[...]
    o_ref[...] = o.T.astype(BF16)


def _alibi_tables(slopes2, tk):
    assert tk <= BF16_INT_RANGE * BF16_INT_RANGE
    r = jnp.arange(tk, dtype=jnp.int32)
    kaug = jnp.zeros((tk, LANES), F32)
    kaug = kaug.at[:, 0:3].set((r % BF16_INT_RANGE).astype(F32)[:, None])
    kaug = kaug.at[:, 3:6].set((r // BF16_INT_RANGE).astype(F32)[:, None])
    s_a = slopes2.astype(BF16).astype(F32)
    s_b = (slopes2 - s_a).astype(BF16).astype(F32)
    s_c = (slopes2 - s_a - s_b).astype(BF16).astype(F32)
    pieces = jnp.stack([s_a, s_b, s_c], axis=1)
    tab = jnp.zeros((slopes2.shape[0], LANES), F32)
    tab = tab.at[:, 0:3].set(pieces).at[:, 3:6].set(float(BF16_INT_RANGE) * pieces)
    dist = jnp.abs(r[:, None] - r[None, :]).astype(F32)
    return tab, kaug.astype(BF16), dist


def _diff_attn(proj, vt_all, slopes2, tab, kaug, dist, lam_l, g_col, *, row0, batch, s_len, tq, tk, lam_init):
    assert tq == tk
    slab = vt_all.shape[2]
    acc_rows = HEAD_DIM + (ONES_ROWS if s_len // tk < SUM_ON_MXU_BELOW_TILES else 0)
    nq = s_len // tq
    qb0 = row0 // tq
    kb0 = row0 // s_len
    n_kh = DQ // HEAD_DIM
    return pl.pallas_call(
        functools.partial(_diff_attn_kernel, s_len=s_len, tq=tq, tk=tk, lam_init=lam_init),
        out_shape=jax.ShapeDtypeStruct((batch * s_len, DQ), BF16),
        grid=(batch, N_DIFF_HEADS, nq),
        in_specs=[
            pl.BlockSpec(memory_space=pltpu.SMEM),
            pl.BlockSpec(tab.shape, lambda b, h, j: (0, 0)),
            pl.BlockSpec((tq, HEAD_DIM), lambda b, h, j: (qb0 + b * nq + j, h)),
            pl.BlockSpec((s_len, HEAD_DIM), lambda b, h, j: (kb0 + b, n_kh + h)),
            pl.BlockSpec((tk, LANES), lambda b, h, j: (0, 0)),
            pl.BlockSpec((tk, tq), lambda b, h, j: (0, 0), pipeline_mode=pl.Buffered(1)),
            pl.BlockSpec((s_len // slab, HEAD_DIM, slab), lambda b, h, j: (kb0 + b, h, 0)),
            pl.BlockSpec((4, DIFF_HALF), lambda b, h, j: (0, 0)),
            pl.BlockSpec((HEAD_DIM, 1), lambda b, h, j: (0, 0)),
        ],
        out_specs=pl.BlockSpec((tq, HEAD_DIM), lambda b, h, j: (b * nq + j, h)),
        scratch_shapes=[pltpu.VMEM((acc_rows, tq), F32)] * 2 + [pltpu.VMEM((tk, tq), F32)] * 4,
        compiler_params=_cparams(3),
        name="diff_attn",
    )(slopes2, tab, proj, proj, kaug, dist, vt_all, lam_l, g_col)


def _win_attn_kernel(sl_ref, sink_ref, q_ref, k_ref, v_ref, o_ref, *, s_len, tq):
    kv = pl.program_id(1)
    j = pl.program_id(2)
    w = tq + 2 * WINDOW
    ws = pl.multiple_of(jnp.clip(j * tq - WINDOW, 0, s_len - w), WINDOW)
    k = k_ref[pl.ds(ws, w), :]
    v = v_ref[pl.ds(ws, w), :]
    qpos = j * tq + lax.broadcasted_iota(jnp.int32, (tq, w), 0)
    kpos = ws + lax.broadcasted_iota(jnp.int32, (tq, w), 1)
    rel = jnp.abs(qpos - kpos)
    relf = jnp.where(rel <= WINDOW, rel.astype(F32), FAR)
    v_ones = jnp.concatenate([v, jnp.ones_like(v)], axis=1)
    for g in range(WIN_GROUP):
        hidx = kv * WIN_GROUP + g
        qg = q_ref[:, g * HEAD_DIM:(g + 1) * HEAD_DIM]
        s = lax.dot_general(qg, k, _NT, preferred_element_type=F32) - sl_ref[hidx] * relf
        sk = sink_ref[hidx]
        m = jnp.maximum(jnp.max(s, axis=1, keepdims=True), sk)
        p = jnp.exp2(s - m).astype(BF16)
        ov = jnp.dot(p, v_ones, preferred_element_type=F32)
        den = ov[:, HEAD_DIM:] + jnp.exp2(sk - m)
        o_ref[:, g * HEAD_DIM:(g + 1) * HEAD_DIM] = (ov[:, :HEAD_DIM] / den).astype(BF16)


def _win_attn(proj, slopes2, sink2, *, row0, batch, s_len, tq):
    nq = s_len // tq
    qb0 = row0 // tq
    kb0 = row0 // s_len
    gw = WIN_GROUP * HEAD_DIM
    q_col0 = (2 * DQ) // gw
    k_col0 = (2 * DQ + WQ) // HEAD_DIM
    v_col0 = (2 * DQ + WQ + WKV) // HEAD_DIM
    return pl.pallas_call(
        functools.partial(_win_attn_kernel, s_len=s_len, tq=tq),
        out_shape=jax.ShapeDtypeStruct((batch * s_len, WQ), BF16),
        grid=(batch, N_WIN_KV, nq),
        in_specs=[
            pl.BlockSpec(memory_space=pltpu.SMEM),
            pl.BlockSpec(memory_space=pltpu.SMEM),
            pl.BlockSpec((tq, gw), lambda b, kv, j: (qb0 + b * nq + j, q_col0 + kv)),
            pl.BlockSpec((s_len, HEAD_DIM), lambda b, kv, j: (kb0 + b, k_col0 + kv)),
            pl.BlockSpec((s_len, HEAD_DIM), lambda b, kv, j: (kb0 + b, v_col0 + kv)),
        ],
        out_specs=pl.BlockSpec((tq, gw), lambda b, kv, j: (b * nq + j, kv)),
        compiler_params=_cparams(3),
        name="win_attn",
    )(slopes2, sink2, proj, proj, proj)


def _pack_rows(v):
    half = v.shape[1] // 2
    lo = lax.bitcast_convert_type(v[:, :half].astype(BF16).astype(F32), jnp.uint32)
    hi = lax.bitcast_convert_type(v[:, half:].astype(BF16).astype(F32), jnp.uint32)
    return (lo >> 16) | hi


def _unpack_rows(u):
    lo = lax.bitcast_convert_type(u << 16, F32)
    hi = lax.bitcast_convert_type(u & jnp.uint32(0xFFFF0000), F32)
    return lo, hi


def _store_token_major(ref, u):
    rows, width = u.shape
    p = width // LANES
    for c in range(p):
        ref[pl.ds(c, rows, stride=p), :] = u[:, c * LANES:(c + 1) * LANES]


def _load_token_major(ref, rows):
    p = ref.shape[0] // rows
    return jnp.concatenate([ref[pl.ds(c, rows, stride=p), :] for c in range(p)], axis=1)


def _layer_norm(z, g, b):
    mu = jnp.mean(z, axis=1, keepdims=True)
    zc = z - mu
    var = jnp.mean(zc * zc, axis=1, keepdims=True)
    return zc * lax.rsqrt(var + LN_EPS) * g + b


def _outproj_kernel(hda_ref, hdb_ref, hwa_ref, hwb_ref, xa_ref, xb_ref, wod_ref, wow_ref,
                    g1_ref, lng_ref, lnb_ref, sc_ref, sh_ref, wr2_ref, wrh_ref, br_ref,
                    x1_ref, h2_ref, er_ref, w0_ref, w1_ref, cnt_ref, *, npb, alpha, tm, n_chunks):
    i = pl.program_id(0)
    first = i < npb
    rows = tm // n_chunks
    p = h2_ref.shape[0] // tm

    @pl.when(i == 0)
    def _():
        cnt_ref[...] = jnp.zeros_like(cnt_ref)

    iota8 = lax.broadcasted_iota(jnp.int32, (SUBLANES, rows), 0)
    iota_e = lax.broadcasted_iota(jnp.int32, (N_EXPERTS, rows), 0)
    upper = jnp.where(lax.broadcasted_iota(jnp.int32, (rows, rows), 0)
                      < lax.broadcasted_iota(jnp.int32, (rows, rows), 1), 1.0, 0.0).astype(BF16)

    def first_argmax(v):
        vmax = jnp.max(v, axis=0, keepdims=True)
        idx = jnp.min(jnp.where(v == vmax, iota8, SUBLANES), axis=0, keepdims=True)
        return vmax, idx

    for c in range(n_chunks):
        rs = slice(c * rows, (c + 1) * rows)
        x = jnp.where(first, xa_ref[rs, :], xb_ref[rs, :])
        hd = jnp.where(first, hda_ref[rs, :], hdb_ref[rs, :])
        hw = jnp.where(first, hwa_ref[rs, :], hwb_ref[rs, :])
        att = (jnp.dot(hd, wod_ref[...], preferred_element_type=F32)
               + jnp.dot(hw, wow_ref[...], preferred_element_type=F32))
        x1 = _layer_norm(alpha * x + g1_ref[...] * att, lng_ref[...], lnb_ref[...])
        x1_ref[rs, :] = x1
        h2 = x1 * (1.0 + sc_ref[...]) + sh_ref[...]
        _store_token_major(h2_ref.at[pl.ds(c * rows * p, rows * p)], _pack_rows(h2))

        h_hi = h2.astype(BF16)
        h_lo = (h2 - h_hi.astype(F32)).astype(BF16)
        both = jnp.dot(h_hi, wr2_ref[...], preferred_element_type=F32)
        lt = (both[:, :LANES] + both[:, LANES:]
              + jnp.dot(h_lo, wrh_ref[...], preferred_element_type=F32))
        lt = lt.T + br_ref[...]

        gl = lt[0:SUBLANES]
        gmax, g_idx = first_argmax(gl)
        g_w = 1.0 / jnp.sum(jnp.exp(gl - gmax), axis=0, keepdims=True)
        el = jnp.zeros((SUBLANES, rows), F32)
        for g in range(N_GROUPS):
            lo = SUBLANES + g * EXPERTS_PER_GROUP
            el = jnp.where(g_idx == g, lt[lo:lo + EXPERTS_PER_GROUP], el)
        v0, i0 = first_argmax(el)
        el2 = jnp.where(iota8 == i0, -jnp.inf, el)
        v1, i1 = first_argmax(el2)
        t = jnp.exp(v1 - v0)
        w0 = g_w / (1.0 + t)
        w1 = g_w * t / (1.0 + t)
        e0 = g_idx * EXPERTS_PER_GROUP + i0
        e1 = g_idx * EXPERTS_PER_GROUP + i1

        hit0 = iota_e == e0
        hit1 = iota_e == e1
        cnt = jnp.where(hit0, 1.0, jnp.where(hit1, 1.0, 0.0))
        before = jnp.dot(cnt.astype(BF16), upper, preferred_element_type=F32)
        tot = before + cnt_ref[:, 0:1]
        r0 = jnp.sum(jnp.where(hit0, tot, 0.0), axis=0, keepdims=True).astype(jnp.int32)
        r1 = jnp.sum(jnp.where(hit1, tot, 0.0), axis=0, keepdims=True).astype(jnp.int32)
        cnt_ref[...] = cnt_ref[...] + jnp.sum(cnt, axis=1, keepdims=True)

        er_ref[:, rs] = jnp.where(iota8 == 0, e0, jnp.where(iota8 == 1, e1,
                                  jnp.where(iota8 == 2, r0, jnp.where(iota8 == 3, r1, 0))))
        w0_ref[rs, :] = jnp.broadcast_to(w0, (LANES, rows)).T
        w1_ref[rs, :] = jnp.broadcast_to(w1, (LANES, rows)).T


def _outproj(lay, hd, hw, xa, xb, same_array, mod3, wo_d, wo_w, ln_g, ln_b, wr_hi, wr_lo, br, alpha, tm):
    d = xa.shape[1]
    p = d // (2 * LANES)
    const = dict(pipeline_mode=pl.Buffered(1))
    vec = lambda: pl.BlockSpec((1, d), lambda i: (0, 0))
    return pl.pallas_call(
        functools.partial(_outproj_kernel, npb=lay.tp // tm, alpha=alpha, tm=tm,
                          n_chunks=OUTPROJ_CHUNKS),
        out_shape=(jax.ShapeDtypeStruct((lay.t, d), F32),
                   jax.ShapeDtypeStruct((lay.t * p, LANES), jnp.uint32),
                   jax.ShapeDtypeStruct((SUBLANES, lay.t), jnp.int32),
                   jax.ShapeDtypeStruct((lay.t, LANES), F32),
                   jax.ShapeDtypeStruct((lay.t, LANES), F32),
                   jax.ShapeDtypeStruct((N_EXPERTS, LANES), F32)),
        grid=(lay.t // tm,),
        in_specs=_x_specs(lay, tm, DQ, False) + _x_specs(lay, tm, WQ, False)
        + _x_specs(lay, tm, d, same_array) + [
            pl.BlockSpec((DQ, d), lambda i: (0, 0), **const),
            pl.BlockSpec((WQ, d), lambda i: (0, 0), **const),
            _mod_spec(lay, tm, 2, d),
            vec(), vec(),
            _mod_spec(lay, tm, 4, d),
            _mod_spec(lay, tm, 3, d),
            pl.BlockSpec((d, 2 * LANES), lambda i: (0, 0)),
            pl.BlockSpec((d, LANES), lambda i: (0, 0)),
            pl.BlockSpec((LANES, 1), lambda i: (0, 0)),
        ],
        out_specs=(pl.BlockSpec((tm, d), lambda i: (i, 0)),
                   pl.BlockSpec((tm * p, LANES), lambda i: (i, 0)),
                   pl.BlockSpec((SUBLANES, tm), lambda i: (0, i)),
                   pl.BlockSpec((tm, LANES), lambda i: (i, 0)),
                   pl.BlockSpec((tm, LANES), lambda i: (i, 0)),
                   pl.BlockSpec((N_EXPERTS, LANES), lambda i: (0, 0))),
        compiler_params=_cparams(1),
        name="out_proj_ln_router",
    )(*hd, *hw, xa, xb, wo_d, wo_w, mod3, ln_g, ln_b, mod3, mod3,
      jnp.concatenate([wr_hi, wr_lo], axis=1), wr_hi, br)


def _token_copy(src, s_tok, dst, d_tok, sem, p):
    return pltpu.make_async_copy(src.at[pl.ds(pl.multiple_of(s_tok * p, p), p)],
                                 dst.at[pl.ds(pl.multiple_of(d_tok * p, p), p)], sem)


def _dispatch_kernel(zrow_ref, pos_ref, h_ref, xs_ref, zbuf, sem, zsem, *, ct, tm, p):
    i = pl.program_id(0)

    def zero_copy(row):
        start = pl.multiple_of(row * p, tm * p)
        return pltpu.make_async_copy(zbuf, xs_ref.at[pl.ds(start, tm * p)], zsem)

    @pl.when(i == 0)
    def _():
        zbuf[...] = jnp.zeros_like(zbuf)
        n_tiles = xs_ref.shape[0] // (tm * p)
        for e in range(N_EXPERTS):
            @pl.when(zrow_ref[e] >= 0)
            def _():
                zero_copy(zrow_ref[e]).start()
        lax.fori_loop(zrow_ref[N_EXPERTS], n_tiles, lambda k, c: (zero_copy(k * tm).start(), c)[1], 0)
        for e in range(N_EXPERTS):
            @pl.when(zrow_ref[e] >= 0)
            def _():
                zero_copy(zrow_ref[e]).wait()
        lax.fori_loop(zrow_ref[N_EXPERTS], n_tiles, lambda k, c: (zero_copy(k * tm).wait(), c)[1], 0)

    def body(j, carry):
        _token_copy(h_ref, j, xs_ref, pos_ref[2 * j], sem, p).start(priority=0)
        _token_copy(h_ref, j, xs_ref, pos_ref[2 * j + 1], sem, p).start(priority=1)
        return carry

    lax.fori_loop(0, ct, body, 0, unroll=8)
    for _ in range(2):
        pltpu.make_async_copy(h_ref, xs_ref.at[pl.ds(0, ct * p)], sem).wait()


def _dispatch(zrow, pos_flat, h2, n_rows, ct, tm, p):
    t = h2.shape[0] // p
    return pl.pallas_call(
        functools.partial(_dispatch_kernel, ct=ct, tm=tm, p=p),
        out_shape=jax.ShapeDtypeStruct((n_rows * p, LANES), jnp.uint32),
        grid=(t // ct,),
        in_specs=[
            pl.BlockSpec(memory_space=pltpu.SMEM),
            pl.BlockSpec((2 * ct,), lambda i: (i,), memory_space=pltpu.SMEM),
            pl.BlockSpec((ct * p, LANES), lambda i: (i, 0)),
        ],
        out_specs=pl.BlockSpec(memory_space=pl.ANY),
        scratch_shapes=[pltpu.VMEM((tm * p, LANES), jnp.uint32),
                        pltpu.SemaphoreType.DMA, pltpu.SemaphoreType.DMA],
        compiler_params=_cparams(1),
        name="moe_dispatch",
    )(zrow, pos_flat, h2)


def _experts_kernel(te_ref, first_ref, slot_ref, nxt_ref, nu_ref, x_ref, wg_hbm, wu_hbm, wd_hbm, o_ref,
                    wg_f, wu_f, wd_f, wg_b, wu_b, wd_b, sems, *, tm, layer):
    i = pl.program_id(0)

    def fetch(e, s):
        return (pltpu.make_async_copy(wg_hbm.at[layer, e], wg_f.at[s], sems.at[0, s]),
                pltpu.make_async_copy(wu_hbm.at[layer, e], wu_f.at[s], sems.at[1, s]),
                pltpu.make_async_copy(wd_hbm.at[layer, e], wd_f.at[s], sems.at[2, s]))

    @pl.when(i < nu_ref[0])
    def _():
        @pl.when(first_ref[i] == 1)
        def _():
            e, s = te_ref[i], slot_ref[i]

            @pl.when(i == 0)
            def _():
                for c in fetch(e, s):
                    c.start()

            for c in fetch(e, s):
                c.wait()
            wg_b[...] = wg_f[s].astype(BF16)
            wu_b[...] = wu_f[s].astype(BF16)
            wd_b[...] = wd_f[s].astype(BF16)

            @pl.when(nxt_ref[i] >= 0)
            def _():
                for c in fetch(nxt_ref[i], 1 - s):
                    c.start()

        lo, hi = _unpack_rows(_load_token_major(x_ref, tm))
        x = jnp.concatenate([lo.astype(BF16), hi.astype(BF16)], axis=1)
        g = jnp.dot(x, wg_b[...], preferred_element_type=F32)
        u = jnp.dot(x, wu_b[...], preferred_element_type=F32)
        a = (g * jax.nn.sigmoid(g) * u).astype(BF16)
        y = jnp.dot(a, wd_b[...], preferred_element_type=F32)
        _store_token_major(o_ref, _pack_rows(y))

    @pl.when(i >= nu_ref[0])
    def _():
        o_ref[...] = jnp.zeros_like(o_ref)


def _experts(tables, xs, wg, wu, wd, layer, tm, p):
    d, f = wg.shape[2], wg.shape[3]
    n_rows = xs.shape[0] // p
    n_tiles = n_rows // tm
    n_pre = len(tables)
    hbm = pl.BlockSpec(memory_space=pl.ANY)

    grid_spec = pltpu.PrefetchScalarGridSpec(
        num_scalar_prefetch=n_pre,
        grid=(n_tiles,),
        in_specs=[
            pl.BlockSpec((tm * p, LANES), lambda i, *pre: (jnp.minimum(i, pre[-1][0] - 1), 0)),
            hbm, hbm, hbm,
        ],
        out_specs=pl.BlockSpec((tm * p, LANES), lambda i, *pre: (i, 0)),
        scratch_shapes=[pltpu.VMEM((2, d, f), F32), pltpu.VMEM((2, d, f), F32), pltpu.VMEM((2, f, d), F32),
                        pltpu.VMEM((d, f), BF16), pltpu.VMEM((d, f), BF16), pltpu.VMEM((f, d), BF16),
                        pltpu.SemaphoreType.DMA((3, 2))],
    )
    return pl.pallas_call(
        functools.partial(_experts_kernel, tm=tm, layer=layer),
        out_shape=jax.ShapeDtypeStruct((n_rows * p, LANES), jnp.uint32),
        grid_spec=grid_spec,
        compiler_params=_cparams(1),
        name="moe_experts",
    )(*tables, xs, wg, wu, wd)


def _final_kernel(posc_ref, posn_ref, x1_ref, ys_ref, w0_ref, w1_ref, g2_ref, lng_ref, lnb_ref, o_ref,
                  ya0, yb0, ya1, yb1, sems, *, alpha, tm, p, n_steps):
    i = pl.program_id(0)
    bufs = ((ya0, yb0), (ya1, yb1))

    def gather(pos_ref, s):
        def body(j, carry):
            _token_copy(ys_ref, pos_ref[2 * j], bufs[s][0], j, sems.at[0, s], p).start(priority=0)
            _token_copy(ys_ref, pos_ref[2 * j + 1], bufs[s][1], j, sems.at[1, s], p).start(priority=1)
            return carry
        lax.fori_loop(0, tm, body, 0, unroll=8)

    def wait(s):
        for k in range(2):
            pltpu.make_async_copy(ys_ref.at[pl.ds(0, tm * p)], bufs[s][k], sems.at[k, s]).wait()

    @pl.when(i == 0)
    def _():
        gather(posc_ref, 0)

    reps = x1_ref.shape[1] // LANES
    for s in range(2):
        @pl.when(i % 2 == s)
        def _():
            @pl.when(i + 1 < n_steps)
            def _():
                gather(posn_ref, 1 - s)
            wait(s)
            w0 = jnp.concatenate([w0_ref[...]] * reps, axis=1)
            w1 = jnp.concatenate([w1_ref[...]] * reps, axis=1)
            ya = jnp.concatenate(_unpack_rows(_load_token_major(bufs[s][0], tm)), axis=1)
            yb = jnp.concatenate(_unpack_rows(_load_token_major(bufs[s][1], tm)), axis=1)
            y = w0 * ya + w1 * yb
            o_ref[...] = _layer_norm(alpha * x1_ref[...] + g2_ref[...] * y, lng_ref[...], lnb_ref[...])


def _final(lay, pos_flat, x1, ys, w0, w1, mod3, ln_g, ln_b, alpha, tm, p, row0, n_rows):
    d = x1.shape[1]
    b0 = row0 // tm
    n_steps = n_rows // tm
    last = lay.t // tm - 1
    row = lambda i: (b0 + i, 0)
    vec = lambda: pl.BlockSpec((1, d), lambda i: (0, 0))
    slab = pltpu.VMEM((tm * p, LANES), jnp.uint32)
    return pl.pallas_call(
        functools.partial(_final_kernel, alpha=alpha, tm=tm, p=p, n_steps=n_steps),
        out_shape=jax.ShapeDtypeStruct((n_rows, d), F32),
        grid=(n_steps,),
        in_specs=[
            pl.BlockSpec((2 * tm,), lambda i: (b0 + i,), memory_space=pltpu.SMEM),
            pl.BlockSpec((2 * tm,), lambda i: (jnp.minimum(b0 + i + 1, last),), memory_space=pltpu.SMEM),
            pl.BlockSpec((tm, d), row),
            pl.BlockSpec(memory_space=pl.ANY),
            pl.BlockSpec((tm, LANES), row), pl.BlockSpec((tm, LANES), row),
            pl.BlockSpec((None, 1, d), lambda i: (lay.batch_of_tile(b0 + i, tm), 0, 5)),
            vec(), vec(),
        ],
        out_specs=pl.BlockSpec((tm, d), lambda i: (i, 0)),
        scratch_shapes=[slab, slab, slab, slab, pltpu.SemaphoreType.DMA((2, 2))],
        compiler_params=_cparams(1),
        name="moe_combine_ln",
    )(pos_flat, pos_flat, x1, ys, w0, w1, mod3, ln_g, ln_b)


def _routing_tables(er, cnt, tm, n_tiles):
    counts = cnt[:, 0].astype(jnp.int32)
    padded = ((counts + tm - 1) // tm) * tm
    ends = jnp.cumsum(padded)
    offs = ends - padded
    pos0 = offs[er[0]] + er[2]
    pos1 = offs[er[1]] + er[3]
    pos_flat = jnp.stack([pos0, pos1], axis=1).reshape(-1)
    tile_start = jnp.arange(n_tiles, dtype=jnp.int32) * tm
    tile_expert = jnp.minimum(
        jnp.sum((tile_start[:, None] >= ends[None, :]).astype(jnp.int32), axis=1), N_EXPERTS - 1)
    n_used = (ends[-1] // tm).reshape(1)
    zrow = jnp.concatenate([jnp.where(counts > 0, ends - tm, -1), n_used])
    tile_expert = tile_expert.astype(jnp.int32)
    used = counts > 0
    ids = jnp.arange(N_EXPERTS, dtype=jnp.int32)
    later = jnp.where((ids[None, :] > ids[:, None]) & used[None, :], ids[None, :], N_EXPERTS)
    next_used = jnp.min(later, axis=1)
    next_used = jnp.where(next_used == N_EXPERTS, -1, next_used).astype(jnp.int32)
    first = jnp.concatenate([jnp.ones((1,), jnp.int32),
                             (tile_expert[1:] != tile_expert[:-1]).astype(jnp.int32)])
    slot = (jnp.cumsum(first) - 1) % 2
    tables = (tile_expert, first, slot.astype(jnp.int32), next_used[tile_expert], n_used.astype(jnp.int32))
    return pos_flat, tables, zrow.astype(jnp.int32)


def kernel(x_prompt, x_sample, c_prompt, c_sample, w_ada, b_ada, w_in, lam, subln_g, sink,
           w_o, ln_g, ln_b, w_rg, b_rg, w_re, b_re, w_gate, w_up, w_down):
    bp, sp, d = x_prompt.shape
    bs, ss, _ = x_sample.shape
    depth = w_ada.shape[0]
    lay = _Layout(bp, sp, bs, ss)
    alpha = (2.0 * depth) ** 0.25
    assert d % (2 * LANES * SUBLANES) == 0 and lay.tp % ss == 0
    p = d // (2 * LANES)

    tm = _pick(math.gcd(sp, ss), ROW_TILE)
    tq_d = tk_d = _pick(math.gcd(sp, ss), DIFF_TILE)
    tq_w = _pick(math.gcd(sp, ss), WIN_TILE)
    while tq_w + 2 * WINDOW > min(sp, ss):
        tq_w //= 2
    ct = _pick(lay.t, DISPATCH_TOKENS)
    n_tiles = (2 * lay.t + N_EXPERTS * (tm - 1) + tm - 1) // tm
    n_rows = n_tiles * tm

    xa = x_prompt.reshape(lay.tp, d)
    xb = x_sample.reshape(bs * ss, d)

    nb = bp + bs
    c_pad = jnp.zeros((-(-nb // SUBLANES) * SUBLANES, d), F32)
    c_pad = c_pad.at[:bp].set(c_prompt).at[bp:nb].set(c_sample)
    mod = _adaln(c_pad, w_ada, b_ada)

    slopes_d = jnp.asarray([LOG2E * 2.0 ** (-8.0 * (h + 1) / N_DIFF_HEADS) for h in range(N_DIFF_HEADS)], F32)
    slopes_w = jnp.asarray([LOG2E * 2.0 ** (-8.0 * (h + 1) / N_WIN_HEADS) for h in range(N_WIN_HEADS)], F32)
    tab_d, kaug_d, dist_d = _alibi_tables(slopes_d, tk_d)
    cs = jnp.ones((2 * DQ + WQ + 2 * WKV,), F32)
    cs = cs.at[:DQ].set(LOG2E * DIFF_HALF ** -0.5).at[2 * DQ:2 * DQ + WQ].set(LOG2E * HEAD_DIM ** -0.5)
    cs = cs.reshape(1, -1)

    same = False
    x_last = None
    for l in range(depth):
        lam_init = 0.8 - 0.6 * math.exp(-0.3 * l)
        mod3 = mod[l].reshape(-1, 1, 6 * d)
        wl = w_in[l]
        w_main = jnp.concatenate([wl[:, :2 * DQ], wl[:, 3 * DQ:]], axis=1).astype(BF16)
        wvt = wl[:, 2 * DQ:3 * DQ].T.astype(BF16)
        proj, vt_all = _inproj(lay, xa, xb, same, mod3, w_main, wvt, cs, tm)

        g_col = subln_g[l].reshape(HEAD_DIM, 1)
        groups = (dict(row0=0, batch=bp, s_len=sp), dict(row0=lay.tp, batch=bs, s_len=ss))
        hd = [_diff_attn(proj, vt_all, slopes_d, tab_d, kaug_d, dist_d, lam[l], g_col, lam_init=lam_init,
                         tq=tq_d, tk=tk_d, **g) for g in groups]
        sink2 = sink[l].astype(F32) * LOG2E
        hw = [_win_attn(proj, slopes_w, sink2, tq=tq_w, **g) for g in groups]

        wr = jnp.zeros((d, LANES), F32)
        wr = wr.at[:, :N_GROUPS].set(w_rg[l])
        wr = wr.at[:, SUBLANES:SUBLANES + N_EXPERTS].set(
            jnp.transpose(w_re[l], (1, 0, 2)).reshape(d, N_EXPERTS))
        wr_hi = wr.astype(BF16)
        wr_lo = (wr - wr_hi.astype(F32)).astype(BF16)
        br = jnp.full((LANES,), NEG_BIG, F32)
        br = br.at[:N_GROUPS].set(b_rg[l]).at[SUBLANES:SUBLANES + N_EXPERTS].set(b_re[l].reshape(-1))
        br = br.reshape(LANES, 1)

        wo = w_o[l].astype(BF16)
        x1, h2, er, w0, w1, cnt = _outproj(
            lay, hd, hw, xa, xb, same, mod3, wo[:DQ], wo[DQ:], ln_g[l, 0].reshape(1, d),
            ln_b[l, 0].reshape(1, d), wr_hi, wr_lo, br, alpha, tm)

        pos_flat, expert_tables, zrow = _routing_tables(er, cnt, tm, n_tiles)
        xs = _dispatch(zrow, pos_flat, h2, n_rows, ct, tm, p)
        ys = _experts(expert_tables, xs, w_gate, w_up, w_down, l, tm, p)
        fin = functools.partial(_final, lay, pos_flat, x1, ys, w0, w1, mod3, ln_g[l, 1].reshape(1, d),
                                ln_b[l, 1].reshape(1, d), alpha, tm, p)
        if l + 1 < depth:
            x_all = fin(0, lay.t)
            xa = xb = x_all
            same = True
        else:
            x_last = (fin(0, lay.tp), fin(lay.tp, lay.t - lay.tp))

    return (x_last[0].reshape(bp, sp, d), x_last[1].reshape(bs, ss, d))
```

```python
import functools
import math

import jax
import jax.numpy as jnp
from jax import lax
from jax.experimental import pallas as pl
from jax.experimental.pallas import tpu as pltpu

F32 = jnp.float32
BF16 = jnp.bfloat16

HEAD_DIM = 128
DIFF_HALF = HEAD_DIM // 2
N_DIFF_HEADS = 8
N_WIN_HEADS = 8
N_WIN_KV = 2
WIN_GROUP = N_WIN_HEADS // N_WIN_KV
WINDOW = 128
DQ = N_DIFF_HEADS * HEAD_DIM
WQ = N_WIN_HEADS * HEAD_DIM
WKV = N_WIN_KV * HEAD_DIM
N_GROUPS = 4
EXPERTS_PER_GROUP = 8
N_EXPERTS = N_GROUPS * EXPERTS_PER_GROUP
LN_EPS = 1e-5
RMS_EPS = 1e-5
LOG2E = 1.4426950408889634
NEG_BIG = -1e30
FAR = 1e30

V7X_VMEM_BYTES = 64 * 1024 * 1024
VMEM_LIMIT = V7X_VMEM_BYTES - 8 * 1024 * 1024
LANES = 128
SUBLANES = 8
ROW_TILE = 512
WIN_TILE = 512
DIFF_TILE = 1024
ADALN_COLS = 1024
DISPATCH_TOKENS = 2048
BF16_INT_RANGE = 256
OUTPROJ_CHUNKS = 2
ONES_ROWS = 16
SUM_ON_MXU_BELOW_TILES = 0

_NT = (((1,), (1,)), ((), ()))


def _cparams(n_axes):
    return pltpu.CompilerParams(
        dimension_semantics=("arbitrary",) * n_axes, vmem_limit_bytes=VMEM_LIMIT)


def _pick(n, pref):
    t = min(pref, n)
    while n % t:
        t //= 2
    return t


def _adaln_kernel(c_ref, w_ref, b_ref, o_ref):
    c = c_ref[...]
    s = (c * jax.nn.sigmoid(c)).astype(BF16)
    o_ref[...] = jnp.dot(s, w_ref[...].astype(BF16), preferred_element_type=F32) + b_ref[...]


def _adaln(c_pad, w_ada, b_ada):
    depth, d, n = w_ada.shape
    tn = _pick(n, ADALN_COLS)
    rows = c_pad.shape[0]
    return pl.pallas_call(
        _adaln_kernel,
        out_shape=jax.ShapeDtypeStruct((depth, rows, n), F32),
        grid=(depth, n // tn),
        in_specs=[
            pl.BlockSpec((rows, d), lambda l, j: (0, 0)),
            pl.BlockSpec((None, d, tn), lambda l, j: (l, 0, j)),
            pl.BlockSpec((None, 1, tn), lambda l, j: (l, 0, j)),
        ],
        out_specs=pl.BlockSpec((None, rows, tn), lambda l, j: (l, 0, j)),
        compiler_params=_cparams(2),
        name="adaln_mod",
    )(c_pad, w_ada, b_ada.reshape(depth, 1, n))


class _Layout:
    def __init__(self, bp, sp, bs, ss):
        self.bp, self.sp, self.bs, self.ss = bp, sp, bs, ss
        self.tp = bp * sp
        self.t = self.tp + bs * ss

    def batch_of_tile(self, i, tm):
        row = i * tm
        return jnp.where(row < self.tp, row // self.sp, self.bp + (row - self.tp) // self.ss)


def _mod_spec(lay, tm, chunk, d):
    return pl.BlockSpec((None, 1, d), lambda i: (lay.batch_of_tile(i, tm), 0, chunk))


def _x_specs(lay, tm, d, same_array):
    npb = lay.tp // tm
    boff = npb if same_array else 0
    return [
        pl.BlockSpec((tm, d), lambda i: (jnp.minimum(i, npb - 1), 0)),
        pl.BlockSpec((tm, d), lambda i: (jnp.maximum(i - npb, 0) + boff, 0)),
    ]


def _select_x(i, npb, xa_ref, xb_ref):
    return jnp.where(i < npb, xa_ref[...], xb_ref[...])


def _inproj_kernel(xa_ref, xb_ref, sc_ref, sh_ref, w_ref, wvt_ref, cs_ref, proj_ref, vt_ref, *, npb):
    x = _select_x(pl.program_id(0), npb, xa_ref, xb_ref)
    h = (x * (1.0 + sc_ref[...]) + sh_ref[...]).astype(BF16)
    acc = jnp.dot(h, w_ref[...], preferred_element_type=F32)
    proj_ref[...] = (acc * cs_ref[...]).astype(BF16)
    vt = lax.dot_general(wvt_ref[...], h, _NT, preferred_element_type=F32)
    vt_ref[...] = vt.astype(BF16)


def _inproj(lay, xa, xb, same_array, mod3, w_main, wvt, col_scale, tm):
    d = xa.shape[1]
    n = w_main.shape[1]
    nv = wvt.shape[0]
    const = dict(pipeline_mode=pl.Buffered(1))
    return pl.pallas_call(
        functools.partial(_inproj_kernel, npb=lay.tp // tm),
        out_shape=(jax.ShapeDtypeStruct((lay.t, n), BF16),
                   jax.ShapeDtypeStruct((lay.t // tm, nv, tm), BF16)),
        grid=(lay.t // tm,),
        in_specs=_x_specs(lay, tm, d, same_array) + [
            _mod_spec(lay, tm, 1, d),
            _mod_spec(lay, tm, 0, d),
            pl.BlockSpec((d, n), lambda i: (0, 0), **const),
            pl.BlockSpec((nv, d), lambda i: (0, 0), **const),
            pl.BlockSpec((1, n), lambda i: (0, 0)),
        ],
        out_specs=(pl.BlockSpec((tm, n), lambda i: (i, 0)),
                   pl.BlockSpec((None, nv, tm), lambda i: (i, 0, 0))),
        compiler_params=_cparams(1),
        name="in_proj",
    )(xa, xb, mod3, mod3, w_main, wvt, col_scale)


def _diff_attn_kernel(sl_ref, tab_ref, q_ref, k_ref, kaug_ref, dist_ref, vt_ref, lam_ref, g_ref, o_ref,
                      acc1, acc2, sa1, sa2, sb1, sb2, *, s_len, tq, tk, lam_init):
    nk = s_len // tk
    sum_on_mxu = acc1.shape[0] > HEAD_DIM
    h = pl.program_id(1)
    j = pl.program_id(2)
    q = q_ref[...]
    lane = lax.broadcasted_iota(jnp.int32, q.shape, 1)
    zero = jnp.zeros_like(q)
    q1 = jnp.where(lane < DIFF_HALF, q, zero)
    q2 = jnp.where(lane >= DIFF_HALF, q, zero)
    slope = sl_ref[h]
    aug = jnp.broadcast_to(tab_ref[pl.ds(h, 1), :], q.shape).astype(BF16)
    qc1 = jnp.concatenate([q1, aug], axis=1)
    qc2 = jnp.concatenate([q2, aug], axis=1)
    kaug = kaug_ref[...]
    kaug_neg = -kaug
    rq = lax.broadcasted_iota(jnp.int32, (1, tq), 1).astype(F32)
    acc1[...] = jnp.zeros_like(acc1)
    acc2[...] = jnp.zeros_like(acc2)

    half = _pick(tq, 2 * LANES)

    def online(s_ref, mx, cq, ml, acc, vt):
        m, l = ml
        m_new = jnp.maximum(m, mx + cq)
        a = jnp.exp2(m - m_new)
        shift = m_new - cq
        sums = []
        for c in range(tq // half):
            cols = slice(c * half, (c + 1) * half)
            p = jnp.exp2(s_ref[:, cols] - shift[:, cols])
            if not sum_on_mxu:
                sums.append(jnp.sum(p, axis=0, keepdims=True))
            acc[:, cols] = (a[:, cols] * acc[:, cols]
                            + jnp.dot(vt, p.astype(BF16), preferred_element_type=F32))
        return m_new, (l if sum_on_mxu else a * l + jnp.concatenate(sums, axis=1))

    def scores(t, buf1, buf2):
        kt = t + (t >= j).astype(jnp.int32)
        before = kt < j
        k = k_ref[pl.ds(pl.multiple_of(kt * tk, tk), tk), :]
        kc = jnp.concatenate([k, jnp.where(before, kaug, kaug_neg)], axis=1)
        s1 = lax.dot_general(kc, qc1, _NT, preferred_element_type=F32)
        s2 = lax.dot_general(kc, qc2, _NT, preferred_element_type=F32)
        buf1[...] = s1
        buf2[...] = s2
        cq = jnp.where(before, -slope, slope) * (rq + (j * tq - kt * tk).astype(F32))
        return kt, cq, jnp.max(s1, axis=0, keepdims=True), jnp.max(s2, axis=0, keepdims=True)

    def scores_diagonal(buf1, buf2):
        k = k_ref[pl.ds(pl.multiple_of(j * tk, tk), tk), :]
        bias = slope * dist_ref[...]
        s1 = lax.dot_general(k, q1, _NT, preferred_element_type=F32) - bias
        s2 = lax.dot_general(k, q2, _NT, preferred_element_type=F32) - bias
        buf1[...] = s1
        buf2[...] = s2
        zer = jnp.zeros((1, tq), F32)
        return j, zer, jnp.max(s1, axis=0, keepdims=True), jnp.max(s2, axis=0, keepdims=True)

    ones = jnp.ones((ONES_ROWS, tk), BF16)

    def consume(tile, buf1, buf2, state):
        kt, cq, mx1, mx2 = tile
        m1, m2 = state
        n_slab = tk // vt_ref.shape[2]
        vt = jnp.concatenate([vt_ref[kt * n_slab + r] for r in range(n_slab)], axis=1)
        if sum_on_mxu:
            vt = jnp.concatenate([vt, ones], axis=0)
        return online(buf1, mx1, cq, m1, acc1, vt), online(buf2, mx2, cq, m2, acc2, vt)

    neg = jnp.full((1, tq), NEG_BIG, F32)
    zer = jnp.zeros((1, tq), F32)
    state = ((neg, zer), (neg, zer))
    tile_a = scores_diagonal(sa1, sa2)
    n_pairs = (nk - 1) // 2

    def body(u, carry):
        state, tile_a = carry
        tile_b = scores(2 * u, sb1, sb2)
        state = consume(tile_a, sa1, sa2, state)
        tile_a = scores(2 * u + 1, sa1, sa2)
        state = consume(tile_b, sb1, sb2, state)
        return state, tile_a

    if n_pairs > 0:
        state, tile_a = lax.fori_loop(0, n_pairs, body, (state, tile_a))
    if nk % 2 == 0:
        tile_b = scores(nk - 2, sb1, sb2)
        state = consume(tile_a, sa1, sa2, state)
        state = consume(tile_b, sb1, sb2, state)
    else:
        state = consume(tile_a, sa1, sa2, state)
    (_, l1), (_, l2) = state

    lv = lam_ref[...]
    lam = (jnp.exp(jnp.sum(lv[0:1] * lv[1:2], axis=1, keepdims=True))
           - jnp.exp(jnp.sum(lv[2:3] * lv[3:4], axis=1, keepdims=True)) + lam_init)
    if sum_on_mxu:
        l1 = acc1[HEAD_DIM:HEAD_DIM + 1, :]
        l2 = acc2[HEAD_DIM:HEAD_DIM + 1, :]
    o = acc1[0:HEAD_DIM, :] / l1 - lam * (acc2[0:HEAD_DIM, :] / l2)
    ms = jnp.mean(o * o, axis=0, keepdims=True)
    o = o * lax.rsqrt(ms + RMS_EPS) * (1.0 - lam_init)
    o = o * g_ref[...]
    o_ref[...] = o.T.astype(BF16)


def _alibi_tables(slopes2, tk):
    assert tk <= BF16_INT_RANGE * BF16_INT_RANGE
    r = jnp.arange(tk, dtype=jnp.int32)
    kaug = jnp.zeros((tk, LANES), F32)
    kaug = kaug.at[:, 0:3].set((r % BF16_INT_RANGE).astype(F32)[:, None])
    kaug = kaug.at[:, 3:6].set((r // BF16_INT_RANGE).astype(F32)[:, None])
    s_a = slopes2.astype(BF16).astype(F32)
    s_b = (slopes2 - s_a).astype(BF16).astype(F32)
    s_c = (slopes2 - s_a - s_b).astype(BF16).astype(F32)
    pieces = jnp.stack([s_a, s_b, s_c], axis=1)
    tab = jnp.zeros((slopes2.shape[0], LANES), F32)
    tab = tab.at[:, 0:3].set(pieces).at[:, 3:6].set(float(BF16_INT_RANGE) * pieces)
    dist = jnp.abs(r[:, None] - r[None, :]).astype(F32)
    return tab, kaug.astype(BF16), dist


def _diff_attn(proj, vt_all, slopes2, tab, kaug, dist, lam_l, g_col, *, row0, batch, s_len, tq, tk, lam_init):
    assert tq == tk
    slab = vt_all.shape[2]
    acc_rows = HEAD_DIM + (ONES_ROWS if s_len // tk < SUM_ON_MXU_BELOW_TILES else 0)
    nq = s_len // tq
    qb0 = row0 // tq
    kb0 = row0 // s_len
    n_kh = DQ // HEAD_DIM
    return pl.pallas_call(
        functools.partial(_diff_attn_kernel, s_len=s_len, tq=tq, tk=tk, lam_init=lam_init),
        out_shape=jax.ShapeDtypeStruct((batch * s_len, DQ), BF16),
        grid=(batch, N_DIFF_HEADS, nq),
        in_specs=[
            pl.BlockSpec(memory_space=pltpu.SMEM),
            pl.BlockSpec(tab.shape, lambda b, h, j: (0, 0)),
            pl.BlockSpec((tq, HEAD_DIM), lambda b, h, j: (qb0 + b * nq + j, h)),
            pl.BlockSpec((s_len, HEAD_DIM), lambda b, h, j: (kb0 + b, n_kh + h)),
            pl.BlockSpec((tk, LANES), lambda b, h, j: (0, 0)),
            pl.BlockSpec((tk, tq), lambda b, h, j: (0, 0), pipeline_mode=pl.Buffered(1)),
            pl.BlockSpec((s_len // slab, HEAD_DIM, slab), lambda b, h, j: (kb0 + b, h, 0)),
            pl.BlockSpec((4, DIFF_HALF), lambda b, h, j: (0, 0)),
            pl.BlockSpec((HEAD_DIM, 1), lambda b, h, j: (0, 0)),
        ],
        out_specs=pl.BlockSpec((tq, HEAD_DIM), lambda b, h, j: (b * nq + j, h)),
        scratch_shapes=[pltpu.VMEM((acc_rows, tq), F32)] * 2 + [pltpu.VMEM((tk, tq), F32)] * 4,
        compiler_params=_cparams(3),
        name="diff_attn",
    )(slopes2, tab, proj, proj, kaug, dist, vt_all, lam_l, g_col)


def _win_attn_kernel(sl_ref, sink_ref, q_ref, k_ref, v_ref, o_ref, *, s_len, tq):
    kv = pl.program_id(1)
    j = pl.program_id(2)
    w = tq + 2 * WINDOW
    ws = pl.multiple_of(jnp.clip(j * tq - WINDOW, 0, s_len - w), WINDOW)
    k = k_ref[pl.ds(ws, w), :]
    v = v_ref[pl.ds(ws, w), :]
    qpos = j * tq + lax.broadcasted_iota(jnp.int32, (tq, w), 0)
    kpos = ws + lax.broadcasted_iota(jnp.int32, (tq, w), 1)
    rel = jnp.abs(qpos - kpos)
    relf = jnp.where(rel <= WINDOW, rel.astype(F32), FAR)
    v_ones = jnp.concatenate([v, jnp.ones_like(v)], axis=1)
    for g in range(WIN_GROUP):
        hidx = kv * WIN_GROUP + g
        qg = q_ref[:, g * HEAD_DIM:(g + 1) * HEAD_DIM]
        s = lax.dot_general(qg, k, _NT, preferred_element_type=F32) - sl_ref[hidx] * relf
        sk = sink_ref[hidx]
        m = jnp.maximum(jnp.max(s, axis=1, keepdims=True), sk)
        p = jnp.exp2(s - m).astype(BF16)
        ov = jnp.dot(p, v_ones, preferred_element_type=F32)
        den = ov[:, HEAD_DIM:] + jnp.exp2(sk - m)
        o_ref[:, g * HEAD_DIM:(g + 1) * HEAD_DIM] = (ov[:, :HEAD_DIM] / den).astype(BF16)


def _win_attn(proj, slopes2, sink2, *, row0, batch, s_len, tq):
    nq = s_len // tq
    qb0 = row0 // tq
    kb0 = row0 // s_len
    gw = WIN_GROUP * HEAD_DIM
    q_col0 = (2 * DQ) // gw
    k_col0 = (2 * DQ + WQ) // HEAD_DIM
    v_col0 = (2 * DQ + WQ + WKV) // HEAD_DIM
    return pl.pallas_call(
        functools.partial(_win_attn_kernel, s_len=s_len, tq=tq),
        out_shape=jax.ShapeDtypeStruct((batch * s_len, WQ), BF16),
        grid=(batch, N_WIN_KV, nq),
        in_specs=[
            pl.BlockSpec(memory_space=pltpu.SMEM),
            pl.BlockSpec(memory_space=pltpu.SMEM),
            pl.BlockSpec((tq, gw), lambda b, kv, j: (qb0 + b * nq + j, q_col0 + kv)),
            pl.BlockSpec((s_len, HEAD_DIM), lambda b, kv, j: (kb0 + b, k_col0 + kv)),
            pl.BlockSpec((s_len, HEAD_DIM), lambda b, kv, j: (kb0 + b, v_col0 + kv)),
        ],
        out_specs=pl.BlockSpec((tq, gw), lambda b, kv, j: (b * nq + j, kv)),
        compiler_params=_cparams(3),
        name="win_attn",
    )(slopes2, sink2, proj, proj, proj)


def _pack_rows(v):
    half = v.shape[1] // 2
    lo = lax.bitcast_convert_type(v[:, :half].astype(BF16).astype(F32), jnp.uint32)
    hi = lax.bitcast_convert_type(v[:, half:].astype(BF16).astype(F32), jnp.uint32)
    return (lo >> 16) | hi


def _unpack_rows(u):
    lo = lax.bitcast_convert_type(u << 16, F32)
    hi = lax.bitcast_convert_type(u & jnp.uint32(0xFFFF0000), F32)
    return lo, hi


def _store_token_major(ref, u):
    rows, width = u.shape
    p = width // LANES
    for c in range(p):
        ref[pl.ds(c, rows, stride=p), :] = u[:, c * LANES:(c + 1) * LANES]


def _load_token_major(ref, rows):
    p = ref.shape[0] // rows
    return jnp.concatenate([ref[pl.ds(c, rows, stride=p), :] for c in range(p)], axis=1)


def _layer_norm(z, g, b):
    mu = jnp.mean(z, axis=1, keepdims=True)
    zc = z - mu
    var = jnp.mean(zc * zc, axis=1, keepdims=True)
    return zc * lax.rsqrt(var + LN_EPS) * g + b


def _outproj_kernel(hda_ref, hdb_ref, hwa_ref, hwb_ref, xa_ref, xb_ref, wod_ref, wow_ref,
                    g1_ref, lng_ref, lnb_ref, sc_ref, sh_ref, wr2_ref, wrh_ref, br_ref,
                    x1_ref, h2_ref, er_ref, w0_ref, w1_ref, cnt_ref, *, npb, alpha, tm, n_chunks):
    i = pl.program_id(0)
    first = i < npb
    rows = tm // n_chunks
    p = h2_ref.shape[0] // tm

    @pl.when(i == 0)
    def _():
        cnt_ref[...] = jnp.zeros_like(cnt_ref)

    iota8 = lax.broadcasted_iota(jnp.int32, (SUBLANES, rows), 0)
    iota_e = lax.broadcasted_iota(jnp.int32, (N_EXPERTS, rows), 0)
    upper = jnp.where(lax.broadcasted_iota(jnp.int32, (rows, rows), 0)
                      < lax.broadcasted_iota(jnp.int32, (rows, rows), 1), 1.0, 0.0).astype(BF16)

    def first_argmax(v):
        vmax = jnp.max(v, axis=0, keepdims=True)
        idx = jnp.min(jnp.where(v == vmax, iota8, SUBLANES), axis=0, keepdims=True)
        return vmax, idx

    for c in range(n_chunks):
        rs = slice(c * rows, (c + 1) * rows)
        x = jnp.where(first, xa_ref[rs, :], xb_ref[rs, :])
        hd = jnp.where(first, hda_ref[rs, :], hdb_ref[rs, :])
        hw = jnp.where(first, hwa_ref[rs, :], hwb_ref[rs, :])
        att = (jnp.dot(hd, wod_ref[...], preferred_element_type=F32)
               + jnp.dot(hw, wow_ref[...], preferred_element_type=F32))
        x1 = _layer_norm(alpha * x + g1_ref[...] * att, lng_ref[...], lnb_ref[...])
        x1_ref[rs, :] = x1
        h2 = x1 * (1.0 + sc_ref[...]) + sh_ref[...]
        _store_token_major(h2_ref.at[pl.ds(c * rows * p, rows * p)], _pack_rows(h2))

        h_hi = h2.astype(BF16)
        h_lo = (h2 - h_hi.astype(F32)).astype(BF16)
        both = jnp.dot(h_hi, wr2_ref[...], preferred_element_type=F32)
        lt = (both[:, :LANES] + both[:, LANES:]
              + jnp.dot(h_lo, wrh_ref[...], preferred_element_type=F32))
        lt = lt.T + br_ref[...]

        gl = lt[0:SUBLANES]
        gmax, g_idx = first_argmax(gl)
        g_w = 1.0 / jnp.sum(jnp.exp(gl - gmax), axis=0, keepdims=True)
        el = jnp.zeros((SUBLANES, rows), F32)
        for g in range(N_GROUPS):
            lo = SUBLANES + g * EXPERTS_PER_GROUP
            el = jnp.where(g_idx == g, lt[lo:lo + EXPERTS_PER_GROUP], el)
        v0, i0 = first_argmax(el)
        el2 = jnp.where(iota8 == i0, -jnp.inf, el)
        v1, i1 = first_argmax(el2)
        t = jnp.exp(v1 - v0)
        w0 = g_w / (1.0 + t)
        w1 = g_w * t / (1.0 + t)
        e0 = g_idx * EXPERTS_PER_GROUP + i0
        e1 = g_idx * EXPERTS_PER_GROUP + i1

        hit0 = iota_e == e0
        hit1 = iota_e == e1
        cnt = jnp.where(hit0, 1.0, jnp.where(hit1, 1.0, 0.0))
        before = jnp.dot(cnt.astype(BF16), upper, preferred_element_type=F32)
        tot = before + cnt_ref[:, 0:1]
        r0 = jnp.sum(jnp.where(hit0, tot, 0.0), axis=0, keepdims=True).astype(jnp.int32)
        r1 = jnp.sum(jnp.where(hit1, tot, 0.0), axis=0, keepdims=True).astype(jnp.int32)
        cnt_ref[...] = cnt_ref[...] + jnp.sum(cnt, axis=1, keepdims=True)

        er_ref[:, rs] = jnp.where(iota8 == 0, e0, jnp.where(iota8 == 1, e1,
                                  jnp.where(iota8 == 2, r0, jnp.where(iota8 == 3, r1, 0))))
        w0_ref[rs, :] = jnp.broadcast_to(w0, (LANES, rows)).T
        w1_ref[rs, :] = jnp.broadcast_to(w1, (LANES, rows)).T


def _outproj(lay, hd, hw, xa, xb, same_array, mod3, wo_d, wo_w, ln_g, ln_b, wr_hi, wr_lo, br, alpha, tm):
    d = xa.shape[1]
    p = d // (2 * LANES)
    const = dict(pipeline_mode=pl.Buffered(1))
    vec = lambda: pl.BlockSpec((1, d), lambda i: (0, 0))
    return pl.pallas_call(
        functools.partial(_outproj_kernel, npb=lay.tp // tm, alpha=alpha, tm=tm,
                          n_chunks=OUTPROJ_CHUNKS),
        out_shape=(jax.ShapeDtypeStruct((lay.t, d), F32),
                   jax.ShapeDtypeStruct((lay.t * p, LANES), jnp.uint32),
                   jax.ShapeDtypeStruct((SUBLANES, lay.t), jnp.int32),
                   jax.ShapeDtypeStruct((lay.t, LANES), F32),
                   jax.ShapeDtypeStruct((lay.t, LANES), F32),
                   jax.ShapeDtypeStruct((N_EXPERTS, LANES), F32)),
        grid=(lay.t // tm,),
        in_specs=_x_specs(lay, tm, DQ, False) + _x_specs(lay, tm, WQ, False)
        + _x_specs(lay, tm, d, same_array) + [
            pl.BlockSpec((DQ, d), lambda i: (0, 0), **const),
            pl.BlockSpec((WQ, d), lambda i: (0, 0), **const),
            _mod_spec(lay, tm, 2, d),
            vec(), vec(),
            _mod_spec(lay, tm, 4, d),
            _mod_spec(lay, tm, 3, d),
            pl.BlockSpec((d, 2 * LANES), lambda i: (0, 0)),
            pl.BlockSpec((d, LANES), lambda i: (0, 0)),
            pl.BlockSpec((LANES, 1), lambda i: (0, 0)),
        ],
        out_specs=(pl.BlockSpec((tm, d), lambda i: (i, 0)),
                   pl.BlockSpec((tm * p, LANES), lambda i: (i, 0)),
                   pl.BlockSpec((SUBLANES, tm), lambda i: (0, i)),
                   pl.BlockSpec((tm, LANES), lambda i: (i, 0)),
                   pl.BlockSpec((tm, LANES), lambda i: (i, 0)),
                   pl.BlockSpec((N_EXPERTS, LANES), lambda i: (0, 0))),
        compiler_params=_cparams(1),
        name="out_proj_ln_router",
    )(*hd, *hw, xa, xb, wo_d, wo_w, mod3, ln_g, ln_b, mod3, mod3,
      jnp.concatenate([wr_hi, wr_lo], axis=1), wr_hi, br)


def _token_copy(src, s_tok, dst, d_tok, sem, p):
    return pltpu.make_async_copy(src.at[pl.ds(pl.multiple_of(s_tok * p, p), p)],
                                 dst.at[pl.ds(pl.multiple_of(d_tok * p, p), p)], sem)


def _dispatch_kernel(zrow_ref, pos_ref, h_ref, xs_ref, zbuf, sem, zsem, *, ct, tm, p):
    i = pl.program_id(0)

    def zero_copy(row):
        start = pl.multiple_of(row * p, tm * p)
        return pltpu.make_async_copy(zbuf, xs_ref.at[pl.ds(start, tm * p)], zsem)

    @pl.when(i == 0)
    def _():
        zbuf[...] = jnp.zeros_like(zbuf)
        n_tiles = xs_ref.shape[0] // (tm * p)
        for e in range(N_EXPERTS):
            @pl.when(zrow_ref[e] >= 0)
            def _():
                zero_copy(zrow_ref[e]).start()
        lax.fori_loop(zrow_ref[N_EXPERTS], n_tiles, lambda k, c: (zero_copy(k * tm).start(), c)[1], 0)
        for e in range(N_EXPERTS):
            @pl.when(zrow_ref[e] >= 0)
            def _():
                zero_copy(zrow_ref[e]).wait()
        lax.fori_loop(zrow_ref[N_EXPERTS], n_tiles, lambda k, c: (zero_copy(k * tm).wait(), c)[1], 0)

    def body(j, carry):
        _token_copy(h_ref, j, xs_ref, pos_ref[2 * j], sem, p).start(priority=0)
        _token_copy(h_ref, j, xs_ref, pos_ref[2 * j + 1], sem, p).start(priority=1)
        return carry

    lax.fori_loop(0, ct, body, 0, unroll=8)
    for _ in range(2):
        pltpu.make_async_copy(h_ref, xs_ref.at[pl.ds(0, ct * p)], sem).wait()


def _dispatch(zrow, pos_flat, h2, n_rows, ct, tm, p):
    t = h2.shape[0] // p
    return pl.pallas_call(
        functools.partial(_dispatch_kernel, ct=ct, tm=tm, p=p),
        out_shape=jax.ShapeDtypeStruct((n_rows * p, LANES), jnp.uint32),
        grid=(t // ct,),
        in_specs=[
            pl.BlockSpec(memory_space=pltpu.SMEM),
            pl.BlockSpec((2 * ct,), lambda i: (i,), memory_space=pltpu.SMEM),
            pl.BlockSpec((ct * p, LANES), lambda i: (i, 0)),
        ],
        out_specs=pl.BlockSpec(memory_space=pl.ANY),
        scratch_shapes=[pltpu.VMEM((tm * p, LANES), jnp.uint32),
                        pltpu.SemaphoreType.DMA, pltpu.SemaphoreType.DMA],
        compiler_params=_cparams(1),
        name="moe_dispatch",
    )(zrow, pos_flat, h2)


def _experts_kernel(te_ref, first_ref, slot_ref, nxt_ref, nu_ref, x_ref, wg_hbm, wu_hbm, wd_hbm, o_ref,
                    wg_f, wu_f, wd_f, wg_b, wu_b, wd_b, sems, *, tm, layer):
    i = pl.program_id(0)

    def fetch(e, s):
        return (pltpu.make_async_copy(wg_hbm.at[layer, e], wg_f.at[s], sems.at[0, s]),
                pltpu.make_async_copy(wu_hbm.at[layer, e], wu_f.at[s], sems.at[1, s]),
                pltpu.make_async_copy(wd_hbm.at[layer, e], wd_f.at[s], sems.at[2, s]))

    @pl.when(i < nu_ref[0])
    def _():
        @pl.when(first_ref[i] == 1)
        def _():
            e, s = te_ref[i], slot_ref[i]

            @pl.when(i == 0)
            def _():
                for c in fetch(e, s):
                    c.start()

            for c in fetch(e, s):
                c.wait()
            wg_b[...] = wg_f[s].astype(BF16)
            wu_b[...] = wu_f[s].astype(BF16)
            wd_b[...] = wd_f[s].astype(BF16)

            @pl.when(nxt_ref[i] >= 0)
            def _():
                for c in fetch(nxt_ref[i], 1 - s):
                    c.start()

        lo, hi = _unpack_rows(_load_token_major(x_ref, tm))
        x = jnp.concatenate([lo.astype(BF16), hi.astype(BF16)], axis=1)
        g = jnp.dot(x, wg_b[...], preferred_element_type=F32)
        u = jnp.dot(x, wu_b[...], preferred_element_type=F32)
        a = (g * jax.nn.sigmoid(g) * u).astype(BF16)
        y = jnp.dot(a, wd_b[...], preferred_element_type=F32)
        _store_token_major(o_ref, _pack_rows(y))

    @pl.when(i >= nu_ref[0])
    def _():
        o_ref[...] = jnp.zeros_like(o_ref)


def _experts(tables, xs, wg, wu, wd, layer, tm, p):
    d, f = wg.shape[2], wg.shape[3]
    n_rows = xs.shape[0] // p
    n_tiles = n_rows // tm
    n_pre = len(tables)
    hbm = pl.BlockSpec(memory_space=pl.ANY)

    grid_spec = pltpu.PrefetchScalarGridSpec(
        num_scalar_prefetch=n_pre,
        grid=(n_tiles,),
        in_specs=[
            pl.BlockSpec((tm * p, LANES), lambda i, *pre: (jnp.minimum(i, pre[-1][0] - 1), 0)),
            hbm, hbm, hbm,
        ],
        out_specs=pl.BlockSpec((tm * p, LANES), lambda i, *pre: (i, 0)),
        scratch_shapes=[pltpu.VMEM((2, d, f), F32), pltpu.VMEM((2, d, f), F32), pltpu.VMEM((2, f, d), F32),
                        pltpu.VMEM((d, f), BF16), pltpu.VMEM((d, f), BF16), pltpu.VMEM((f, d), BF16),
                        pltpu.SemaphoreType.DMA((3, 2))],
    )
    return pl.pallas_call(
        functools.partial(_experts_kernel, tm=tm, layer=layer),
        out_shape=jax.ShapeDtypeStruct((n_rows * p, LANES), jnp.uint32),
        grid_spec=grid_spec,
        compiler_params=_cparams(1),
        name="moe_experts",
    )(*tables, xs, wg, wu, wd)


def _final_kernel(posc_ref, posn_ref, x1_ref, ys_ref, w0_ref, w1_ref, g2_ref, lng_ref, lnb_ref, o_ref,
                  ya0, yb0, ya1, yb1, sems, *, alpha, tm, p, n_steps):
    i = pl.program_id(0)
    bufs = ((ya0, yb0), (ya1, yb1))

    def gather(pos_ref, s):
        def body(j, carry):
            _token_copy(ys_ref, pos_ref[2 * j], bufs[s][0], j, sems.at[0, s], p).start(priority=0)
            _token_copy(ys_ref, pos_ref[2 * j + 1], bufs[s][1], j, sems.at[1, s], p).start(priority=1)
            return carry
        lax.fori_loop(0, tm, body, 0, unroll=8)

    def wait(s):
        for k in range(2):
            pltpu.make_async_copy(ys_ref.at[pl.ds(0, tm * p)], bufs[s][k], sems.at[k, s]).wait()

    @pl.when(i == 0)
    def _():
        gather(posc_ref, 0)

    reps = x1_ref.shape[1] // LANES
    for s in range(2):
        @pl.when(i % 2 == s)
        def _():
            @pl.when(i + 1 < n_steps)
            def _():
                gather(posn_ref, 1 - s)
            wait(s)
            w0 = jnp.concatenate([w0_ref[...]] * reps, axis=1)
            w1 = jnp.concatenate([w1_ref[...]] * reps, axis=1)
            ya = jnp.concatenate(_unpack_rows(_load_token_major(bufs[s][0], tm)), axis=1)
            yb = jnp.concatenate(_unpack_rows(_load_token_major(bufs[s][1], tm)), axis=1)
            y = w0 * ya + w1 * yb
            o_ref[...] = _layer_norm(alpha * x1_ref[...] + g2_ref[...] * y, lng_ref[...], lnb_ref[...])


def _final(lay, pos_flat, x1, ys, w0, w1, mod3, ln_g, ln_b, alpha, tm, p, row0, n_rows):
    d = x1.shape[1]
    b0 = row0 // tm
    n_steps = n_rows // tm
    last = lay.t // tm - 1
    row = lambda i: (b0 + i, 0)
    vec = lambda: pl.BlockSpec((1, d), lambda i: (0, 0))
    slab = pltpu.VMEM((tm * p, LANES), jnp.uint32)
    return pl.pallas_call(
        functools.partial(_final_kernel, alpha=alpha, tm=tm, p=p, n_steps=n_steps),
        out_shape=jax.ShapeDtypeStruct((n_rows, d), F32),
        grid=(n_steps,),
        in_specs=[
            pl.BlockSpec((2 * tm,), lambda i: (b0 + i,), memory_space=pltpu.SMEM),
            pl.BlockSpec((2 * tm,), lambda i: (jnp.minimum(b0 + i + 1, last),), memory_space=pltpu.SMEM),
            pl.BlockSpec((tm, d), row),
            pl.BlockSpec(memory_space=pl.ANY),
            pl.BlockSpec((tm, LANES), row), pl.BlockSpec((tm, LANES), row),
            pl.BlockSpec((None, 1, d), lambda i: (lay.batch_of_tile(b0 + i, tm), 0, 5)),
            vec(), vec(),
        ],
        out_specs=pl.BlockSpec((tm, d), lambda i: (i, 0)),
        scratch_shapes=[slab, slab, slab, slab, pltpu.SemaphoreType.DMA((2, 2))],
        compiler_params=_cparams(1),
        name="moe_combine_ln",
    )(pos_flat, pos_flat, x1, ys, w0, w1, mod3, ln_g, ln_b)


def _routing_tables(er, cnt, tm, n_tiles):
    counts = cnt[:, 0].astype(jnp.int32)
    padded = ((counts + tm - 1) // tm) * tm
    ends = jnp.cumsum(padded)
    offs = ends - padded
    pos0 = offs[er[0]] + er[2]
    pos1 = offs[er[1]] + er[3]
    pos_flat = jnp.stack([pos0, pos1], axis=1).reshape(-1)
    tile_start = jnp.arange(n_tiles, dtype=jnp.int32) * tm
    tile_expert = jnp.minimum(
        jnp.sum((tile_start[:, None] >= ends[None, :]).astype(jnp.int32), axis=1), N_EXPERTS - 1)
    n_used = (ends[-1] // tm).reshape(1)
    zrow = jnp.concatenate([jnp.where(counts > 0, ends - tm, -1), n_used])
    tile_expert = tile_expert.astype(jnp.int32)
    used = counts > 0
    ids = jnp.arange(N_EXPERTS, dtype=jnp.int32)
    later = jnp.where((ids[None, :] > ids[:, None]) & used[None, :], ids[None, :], N_EXPERTS)
    next_used = jnp.min(later, axis=1)
    next_used = jnp.where(next_used == N_EXPERTS, -1, next_used).astype(jnp.int32)
    first = jnp.concatenate([jnp.ones((1,), jnp.int32),
                             (tile_expert[1:] != tile_expert[:-1]).astype(jnp.int32)])
    slot = (jnp.cumsum(first) - 1) % 2
    tables = (tile_expert, first, slot.astype(jnp.int32), next_used[tile_expert], n_used.astype(jnp.int32))
    return pos_flat, tables, zrow.astype(jnp.int32)


def kernel(x_prompt, x_sample, c_prompt, c_sample, w_ada, b_ada, w_in, lam, subln_g, sink,
           w_o, ln_g, ln_b, w_rg, b_rg, w_re, b_re, w_gate, w_up, w_down):
    bp, sp, d = x_prompt.shape
    bs, ss, _ = x_sample.shape
    depth = w_ada.shape[0]
    lay = _Layout(bp, sp, bs, ss)
    alpha = (2.0 * depth) ** 0.25
    assert d % (2 * LANES * SUBLANES) == 0 and lay.tp % ss == 0
    p = d // (2 * LANES)

    tm = _pick(math.gcd(sp, ss), ROW_TILE)
    tq_d = tk_d = _pick(math.gcd(sp, ss), DIFF_TILE)
    tq_w = _pick(math.gcd(sp, ss), WIN_TILE)
    while tq_w + 2 * WINDOW > min(sp, ss):
        tq_w //= 2
    ct = _pick(lay.t, DISPATCH_TOKENS)
    n_tiles = (2 * lay.t + N_EXPERTS * (tm - 1) + tm - 1) // tm
    n_rows = n_tiles * tm

    xa = x_prompt.reshape(lay.tp, d)
    xb = x_sample.reshape(bs * ss, d)

    nb = bp + bs
    c_pad = jnp.zeros((-(-nb // SUBLANES) * SUBLANES, d), F32)
    c_pad = c_pad.at[:bp].set(c_prompt).at[bp:nb].set(c_sample)
    mod = _adaln(c_pad, w_ada, b_ada)

    slopes_d = jnp.asarray([LOG2E * 2.0 ** (-8.0 * (h + 1) / N_DIFF_HEADS) for h in range(N_DIFF_HEADS)], F32)
    slopes_w = jnp.asarray([LOG2E * 2.0 ** (-8.0 * (h + 1) / N_WIN_HEADS) for h in range(N_WIN_HEADS)], F32)
    tab_d, kaug_d, dist_d = _alibi_tables(slopes_d, tk_d)
    cs = jnp.ones((2 * DQ + WQ + 2 * WKV,), F32)
    cs = cs.at[:DQ].set(LOG2E * DIFF_HALF ** -0.5).at[2 * DQ:2 * DQ + WQ].set(LOG2E * HEAD_DIM ** -0.5)
    cs = cs.reshape(1, -1)

    same = False
    x_last = None
    for l in range(depth):
        lam_init = 0.8 - 0.6 * math.exp(-0.3 * l)
        mod3 = mod[l].reshape(-1, 1, 6 * d)
        wl = w_in[l]
        w_main = jnp.concatenate([wl[:, :2 * DQ], wl[:, 3 * DQ:]], axis=1).astype(BF16)
        wvt = wl[:, 2 * DQ:3 * DQ].T.astype(BF16)
        proj, vt_all = _inproj(lay, xa, xb, same, mod3, w_main, wvt, cs, tm)

        g_col = subln_g[l].reshape(HEAD_DIM, 1)
        groups = (dict(row0=0, batch=bp, s_len=sp), dict(row0=lay.tp, batch=bs, s_len=ss))
        hd = [_diff_attn(proj, vt_all, slopes_d, tab_d, kaug_d, dist_d, lam[l], g_col, lam_init=lam_init,
                         tq=tq_d, tk=tk_d, **g) for g in groups]
        sink2 = sink[l].astype(F32) * LOG2E
        hw = [_win_attn(proj, slopes_w, sink2, tq=tq_w, **g) for g in groups]

        wr = jnp.zeros((d, LANES), F32)
        wr = wr.at[:, :N_GROUPS].set(w_rg[l])
        wr = wr.at[:, SUBLANES:SUBLANES + N_EXPERTS].set(
            jnp.transpose(w_re[l], (1, 0, 2)).reshape(d, N_EXPERTS))
        wr_hi = wr.astype(BF16)
        wr_lo = (wr - wr_hi.astype(F32)).astype(BF16)
        br = jnp.full((LANES,), NEG_BIG, F32)
        br = br.at[:N_GROUPS].set(b_rg[l]).at[SUBLANES:SUBLANES + N_EXPERTS].set(b_re[l].reshape(-1))
        br = br.reshape(LANES, 1)

        wo = w_o[l].astype(BF16)
        x1, h2, er, w0, w1, cnt = _outproj(
            lay, hd, hw, xa, xb, same, mod3, wo[:DQ], wo[DQ:], ln_g[l, 0].reshape(1, d),
            ln_b[l, 0].reshape(1, d), wr_hi, wr_lo, br, alpha, tm)

        pos_flat, expert_tables, zrow = _routing_tables(er, cnt, tm, n_tiles)
        xs = _dispatch(zrow, pos_flat, h2, n_rows, ct, tm, p)
        ys = _experts(expert_tables, xs, w_gate, w_up, w_down, l, tm, p)
        fin = functools.partial(_final, lay, pos_flat, x1, ys, w0, w1, mod3, ln_g[l, 1].reshape(1, d),
                                ln_b[l, 1].reshape(1, d), alpha, tm, p)
        if l + 1 < depth:
            x_all = fin(0, lay.t)
            xa = xb = x_all
            same = True
        else:
            x_last = (fin(0, lay.tp), fin(lay.tp, lay.t - lay.tp))

    return (x_last[0].reshape(bp, sp, d), x_last[1].reshape(bs, ss, d))
```
